```python
import math
import jax, jax.numpy as jnp
from jax import lax
import numpy as np

D_MODEL = 4096
BATCH = 1
SEQ = 16384
DEPTH = 2

GRID_W = 64
CTX_LEN = 256
HEAD_DIM = 128
NA_HEADS = 16
NA_WIDTH = NA_HEADS * HEAD_DIM
NA_KR = 8
NA_KW = 16
DIFF_HEADS = 8
DIFF_QK_WIDTH = DIFF_HEADS * 2 * HEAD_DIM
DIFF_WIDTH = DIFF_HEADS * 2 * HEAD_DIM
DIFF_LN_EPS = 1e-5
CONV_WIDTH = 2048
CONV_K = 31
ROPE_BASE = 10000.0
Q_BLOCK = 128
LN_EPS = 1e-6

SEGMENTS = (
    ('na_q', NA_WIDTH), ('na_k', NA_WIDTH), ('na_v', NA_WIDTH), ('na_gate', NA_WIDTH),
    ('df_q', DIFF_QK_WIDTH), ('df_k', DIFF_QK_WIDTH), ('df_v', DIFF_WIDTH), ('df_gate', DIFF_WIDTH),
    ('cv_val', CONV_WIDTH), ('cv_glu', CONV_WIDTH), ('cv_gate', CONV_WIDTH),
    ('merge_na', D_MODEL), ('merge_df', D_MODEL), ('merge_cv', D_MODEL),
)
N_IN = 4 * NA_WIDTH + 2 * DIFF_QK_WIDTH + 2 * DIFF_WIDTH + 3 * CONV_WIDTH + 3 * D_MODEL
CTX_KV_SEGMENTS = ('na_k', 'na_v', 'df_k', 'df_v')

kernel_name = 'hybrid_natten_diffattn_conformer_prefix_block'


def _segment_table():
    table, off = {}, 0
    for name, size in SEGMENTS:
        table[name] = (off, size)
        off += size
    return table


def _layernorm(x, g=None, b=None, eps=LN_EPS):
    xf = x.astype(jnp.float32)
    mu = jnp.mean(xf, -1, keepdims=True)
    var = jnp.mean(jnp.square(xf - mu), -1, keepdims=True)
    y = (xf - mu) * lax.rsqrt(var + eps)
    if g is not None:
        y = y * g.astype(jnp.float32) + b.astype(jnp.float32)
    return y.astype(x.dtype)


def _rmsnorm(x, g, eps):
    xf = x.astype(jnp.float32)
    y = xf * lax.rsqrt(jnp.mean(xf * xf, -1, keepdims=True) + eps)
    return (y * g.astype(jnp.float32)).astype(x.dtype)


def _project_all(h, w, b):
    z = h @ w + b
    return {n: z[..., o:o + s] for n, (o, s) in _segment_table().items()}


def _project_some(h, w, b, names):
    t = _segment_table()
    return {n: h @ w[:, t[n][0]:t[n][0] + t[n][1]] + b[t[n][0]:t[n][0] + t[n][1]] for n in names}


def _heads(t, n_heads):
    return t.reshape(t.shape[0], t.shape[1], n_heads, -1)


def _axial_rope_tables(n_tokens):
    t = jnp.arange(n_tokens, dtype=jnp.int32)
    row = (t // GRID_W).astype(jnp.float32)
    col = (t % GRID_W).astype(jnp.float32)
    n_pairs_axis = HEAD_DIM // 4
    inv_freq = ROPE_BASE ** (-jnp.arange(n_pairs_axis, dtype=jnp.float32) / n_pairs_axis)
    ang = jnp.concatenate([row[:, None] * inv_freq, col[:, None] * inv_freq], -1)
    return jnp.cos(ang), jnp.sin(ang)


def _apply_rope(x, cos, sin):
    xf = x.astype(jnp.float32).reshape(x.shape[:-1] + (HEAD_DIM // 2, 2))
    x1, x2 = xf[..., 0], xf[..., 1]
    cc, ss = cos[None, :, None, :], sin[None, :, None, :]
    out = jnp.stack([x1 * cc - x2 * ss, x1 * ss + x2 * cc], -1).reshape(x.shape)
    return out.astype(x.dtype)


def _dense_attention(q, k, v):
    b, lq, h, dh = q.shape
    s = jnp.einsum('bqhd,bkhd->bhqk', q, k).astype(jnp.float32) * (dh ** -0.5)
    p = jax.nn.softmax(s, -1).astype(v.dtype)
    return jnp.einsum('bhqk,bkhd->bqhd', p, v).reshape(b, lq, -1)


def _neighbourhood_attention(q, k, v, k_ctx, v_ctx, rpb):
    b, l, h, dh = q.shape
    rows = l // GRID_W
    kr = min(NA_KR, rows)
    scale = dh ** -0.5
    qg = q.reshape(b, rows, GRID_W, h, dh)
    kg = k.reshape(b, rows, GRID_W, h, dh)
    vg = v.reshape(b, rows, GRID_W, h, dh)
    cols = jnp.arange(GRID_W)
    col_start = jnp.clip(cols - NA_KW // 2, 0, GRID_W - NA_KW)
    col_idx = col_start[:, None] + jnp.arange(NA_KW)[None, :]
    col_off = col_idx - cols[:, None] + (NA_KW - 1)
    rpb_cols = rpb.astype(jnp.float32)[:, :, col_off]

    def one_row(r):
        r0 = jnp.clip(r - kr // 2, 0, rows - kr)
        q_r = lax.dynamic_index_in_dim(qg, r, axis=1, keepdims=False)
        k_win = lax.dynamic_slice_in_dim(kg, r0, kr, axis=1)[:, :, col_idx]
        v_win = lax.dynamic_slice_in_dim(vg, r0, kr, axis=1)[:, :, col_idx]
        row_off = r0 + jnp.arange(kr) - r + (NA_KR - 1)
        bias = jnp.transpose(rpb_cols[:, row_off], (0, 2, 1, 3))
        s_loc = jnp.einsum('bjhd,brjwhd->bhjrw', q_r, k_win).astype(jnp.float32) * scale + bias[None]
        s_ctx = jnp.einsum('bjhd,bchd->bhjc', q_r, k_ctx).astype(jnp.float32) * scale
        s = jnp.concatenate([s_loc.reshape(b, h, GRID_W, kr * NA_KW), s_ctx], -1)
        p = jax.nn.softmax(s, -1).astype(v.dtype)
        p_loc = p[..., :kr * NA_KW].reshape(b, h, GRID_W, kr, NA_KW)
        p_ctx = p[..., kr * NA_KW:]
        return (jnp.einsum('bhjrw,brjwhd->bjhd', p_loc, v_win)
                + jnp.einsum('bhjc,bchd->bjhd', p_ctx, v_ctx))

    out = lax.map(one_row, jnp.arange(rows))
    return jnp.transpose(out, (1, 0, 2, 3, 4)).reshape(b, l, h * dh)


def _diff_attention(q, k, v, lam, lam_init, subln_g):
    b, lq, h, _, dh = q.shape
    nb = lq // Q_BLOCK
    scale = dh ** -0.5
    qb = jnp.transpose(q.reshape(b, nb, Q_BLOCK, h, 2, dh), (1, 0, 2, 3, 4, 5))

    def one_block(q_blk):
        s = jnp.einsum('bqhmd,bkhmd->bhmqk', q_blk, k).astype(jnp.float32) * scale
        p = jax.nn.softmax(s, -1)
        w = (p[:, :, 0] - lam * p[:, :, 1]).astype(v.dtype)
        o = jnp.einsum('bhqk,bkhe->bqhe', w, v)
        return _rmsnorm(o, subln_g, DIFF_LN_EPS) * (1.0 - lam_init)

    out = lax.map(one_block, qb)
    return jnp.transpose(out, (1, 0, 2, 3, 4)).reshape(b, lq, -1)


def _conformer_conv(val, glu, w, bias, g, beta):
    u = val * jax.nn.sigmoid(glu)
    ch = u.shape[-1]
    y = lax.conv_general_dilated(u, w[:, None, :].astype(u.dtype), window_strides=(1,),
                                 padding=[(CONV_K // 2, CONV_K // 2)],
                                 dimension_numbers=('NWC', 'WIO', 'NWC'),
                                 feature_group_count=ch) + bias
    return jax.nn.silu(_layernorm(y, g, beta, eps=1e-5))


def _merge(p, o_na, o_df, o_cv, w_na, w_df, w_cv, w_o):
    y_na = (o_na * jax.nn.silu(p['na_gate'])) @ w_na
    y_df = (o_df * jax.nn.silu(p['df_gate'])) @ w_df
    y_cv = (o_cv * jax.nn.silu(p['cv_gate'])) @ w_cv
    y = (jax.nn.sigmoid(p['merge_na']) * y_na + jax.nn.sigmoid(p['merge_df']) * y_df
         + jax.nn.sigmoid(p['merge_cv']) * y_cv)
    return y @ w_o


def setup_inputs(seed: int = 0) -> dict:
    key = jax.random.key(seed)
    ks = jax.random.split(key, 26)
    d = D_MODEL
    beta = (8.0 * DEPTH) ** -0.25

    def nrm(k, shape, s):
        return jax.random.normal(k, shape, jnp.float32) * s

    t = _segment_table()
    col_scale = np.ones((N_IN,), np.float32)
    for n in ('na_v', 'df_v'):
        o, s = t[n]
        col_scale[o:o + s] = beta
    return {
        'x': nrm(ks[0], (BATCH, SEQ, d), 1.0),
        'c': nrm(ks[1], (BATCH, d), 1.0),
        'ctx': nrm(ks[2], (BATCH, CTX_LEN, d), 1.0),
        'c_ctx': nrm(ks[3], (d,), 1.0),
        'w_ada': nrm(ks[4], (DEPTH, d, 3 * d), 0.5 * d ** -0.5),
        'b_ada': nrm(ks[5], (DEPTH, 3 * d), 0.02),
        'w_in': nrm(ks[6], (DEPTH, d, N_IN), d ** -0.5) * jnp.asarray(col_scale),
        'b_in': nrm(ks[7], (DEPTH, N_IN), 0.02),
        'na_rpb': nrm(ks[8], (DEPTH, NA_HEADS, 2 * NA_KR - 1, 2 * NA_KW - 1), 0.1),
        'diff_lq1': nrm(ks[9], (DEPTH, HEAD_DIM), 0.1),
        'diff_lk1': nrm(ks[10], (DEPTH, HEAD_DIM), 0.1),
        'diff_lq2': nrm(ks[11], (DEPTH, HEAD_DIM), 0.1),
        'diff_lk2': nrm(ks[12], (DEPTH, HEAD_DIM), 0.1),
        'diff_subln_g': 1.0 + nrm(ks[13], (DEPTH, 2 * HEAD_DIM), 0.02),
        'conv_w': nrm(ks[14], (DEPTH, CONV_K, CONV_WIDTH), CONV_K ** -0.5),
        'conv_b': nrm(ks[15], (DEPTH, CONV_WIDTH), 0.02),
        'conv_ln_g': 1.0 + nrm(ks[16], (DEPTH, CONV_WIDTH), 0.02),
        'conv_ln_b': nrm(ks[17], (DEPTH, CONV_WIDTH), 0.02),
        'w_proj_na': nrm(ks[18], (DEPTH, NA_WIDTH, d), beta * NA_WIDTH ** -0.5),
        'w_proj_diff': nrm(ks[19], (DEPTH, DIFF_WIDTH, d), beta * DIFF_WIDTH ** -0.5),
        'w_proj_conv': nrm(ks[20], (DEPTH, CONV_WIDTH, d), beta * CONV_WIDTH ** -0.5),
        'w_out': nrm(ks[21], (DEPTH, d, d), beta * d ** -0.5),
        'post_ln_g': 1.0 + nrm(ks[22], (DEPTH, d), 0.02),
        'post_ln_b': nrm(ks[23], (DEPTH, d), 0.02),
    }


def reference(x, c, ctx, c_ctx, w_ada, b_ada, w_in, b_in, na_rpb, diff_lq1, diff_lk1, diff_lq2,
              diff_lk2, diff_subln_g, conv_w, conv_b, conv_ln_g, conv_ln_b, w_proj_na, w_proj_diff,
              w_proj_conv, w_out, post_ln_g, post_ln_b):
    b, l, _ = x.shape
    n_ctx = ctx.shape[1]
    alpha = (2.0 * DEPTH) ** 0.25
    cos, sin = _axial_rope_tables(l)
    for i in range(DEPTH):
        last = i == DEPTH - 1
        shift, scale, gate = jnp.split(jax.nn.silu(c) @ w_ada[i] + b_ada[i], 3, axis=-1)
        shift_c, scale_c, gate_c = jnp.split(jax.nn.silu(c_ctx) @ w_ada[i] + b_ada[i], 3, axis=-1)
        h = _layernorm(x) * (1.0 + scale[:, None, :]) + shift[:, None, :]
        hc = _layernorm(ctx) * (1.0 + scale_c) + shift_c
        p = _project_all(h, w_in[i], b_in[i])
        pc = (_project_some(hc, w_in[i], b_in[i], CTX_KV_SEGMENTS) if last
              else _project_all(hc, w_in[i], b_in[i]))

        na_kc, na_vc = _heads(pc['na_k'], NA_HEADS), _heads(pc['na_v'], NA_HEADS)
        o_na = _neighbourhood_attention(_heads(p['na_q'], NA_HEADS), _heads(p['na_k'], NA_HEADS),
                                        _heads(p['na_v'], NA_HEADS), na_kc, na_vc, na_rpb[i])

        df_q = _apply_rope(_heads(p['df_q'], 2 * DIFF_HEADS), cos, sin).reshape(b, l, DIFF_HEADS, 2, HEAD_DIM)
        df_k = _apply_rope(_heads(p['df_k'], 2 * DIFF_HEADS), cos, sin).reshape(b, l, DIFF_HEADS, 2, HEAD_DIM)
        df_kc = pc['df_k'].reshape(b, n_ctx, DIFF_HEADS, 2, HEAD_DIM)
        df_vc = _heads(pc['df_v'], DIFF_HEADS)
        lam_init = 0.8 - 0.6 * math.exp(-0.3 * i)
        lam = (jnp.exp(jnp.sum(diff_lq1[i].astype(jnp.float32) * diff_lk1[i].astype(jnp.float32)))
               - jnp.exp(jnp.sum(diff_lq2[i].astype(jnp.float32) * diff_lk2[i].astype(jnp.float32)))
               + lam_init)
        o_df = _diff_attention(df_q, jnp.concatenate([df_k, df_kc], 1),
                               jnp.concatenate([_heads(p['df_v'], DIFF_HEADS), df_vc], 1),
                               lam, lam_init, diff_subln_g[i])

        o_cv = _conformer_conv(p['cv_val'], p['cv_glu'], conv_w[i], conv_b[i], conv_ln_g[i], conv_ln_b[i])

        y = _merge(p, o_na, o_df, o_cv, w_proj_na[i], w_proj_diff[i], w_proj_conv[i], w_out[i])
        x_new = _layernorm(alpha * x + gate[:, None, :] * y, post_ln_g[i], post_ln_b[i])

        if not last:
            oc_na = _dense_attention(_heads(pc['na_q'], NA_HEADS), na_kc, na_vc)
            oc_df = _diff_attention(pc['df_q'].reshape(b, n_ctx, DIFF_HEADS, 2, HEAD_DIM), df_kc, df_vc,
                                    lam, lam_init, diff_subln_g[i])
            oc_cv = _conformer_conv(pc['cv_val'], pc['cv_glu'], conv_w[i], conv_b[i], conv_ln_g[i], conv_ln_b[i])
            yc = _merge(pc, oc_na, oc_df, oc_cv, w_proj_na[i], w_proj_diff[i], w_proj_conv[i], w_out[i])
            ctx = _layernorm(alpha * ctx + gate_c * yc, post_ln_g[i], post_ln_b[i])
        x = x_new
    return x
```

```python
import functools
import math

import jax
import jax.numpy as jnp
from jax import lax
from jax.experimental import pallas as pl
from jax.experimental.pallas import tpu as pltpu

HEAD_DIM = 128
GRID_W = 64
NA_KR = 8
NA_KW = 16
CONV_K = 31
CONV_HALO = 16
ROPE_BASE = 10000.0
LN_EPS = 1e-6
DIFF_LN_EPS = 1e-5
CONV_LN_EPS = 1e-5
NEG_BIG = -1e30
V7X_VMEM_LIMIT = 56 * 1024 * 1024

F32 = jnp.float32
BF16 = jnp.bfloat16


def _tile(n, pref):
    t = min(n, pref)
    while n % t:
        t //= 2
    return t


def _params(sem):
    return pltpu.CompilerParams(dimension_semantics=sem, vmem_limit_bytes=V7X_VMEM_LIMIT)


def _silu(x):
    return x * jax.nn.sigmoid(x)


def _ln_rows(x, eps):
    mu = jnp.mean(x, -1, keepdims=True)
    xc = x - mu
    var = jnp.mean(xc * xc, -1, keepdims=True)
    return xc * lax.rsqrt(var + eps)


def _dot_nt(a, b):
    return lax.dot_general(a, b, (((1,), (1,)), ((), ())), preferred_element_type=F32)


def _ada_kernel(cc_ref, w_ref, b_ref, o_ref):
    s = _silu(cc_ref[...])
    o_ref[0] = jnp.dot(s.astype(BF16), w_ref[0].astype(BF16), preferred_element_type=F32) + b_ref[0]


def _ada(cc, w_ada, b_ada):
    depth, d, n = w_ada.shape
    tn = _tile(n, 512)
    return pl.pallas_call(
        _ada_kernel,
        grid=(depth, n // tn),
        in_specs=[pl.BlockSpec((8, d), lambda i, j: (0, 0)),
                  pl.BlockSpec((1, d, tn), lambda i, j: (i, 0, j)),
                  pl.BlockSpec((1, 1, tn), lambda i, j: (i, 0, j))],
        out_specs=pl.BlockSpec((1, 8, tn), lambda i, j: (i, 0, j)),
        out_shape=jax.ShapeDtypeStruct((depth, 8, n), F32),
        compiler_params=_params(("arbitrary", "arbitrary")),
        name="ada",
    )(cc, w_ada, b_ada.reshape(depth, 1, n))


def _lnmod_kernel(x_ref, sc_ref, sh_ref, o_ref):
    y = _ln_rows(x_ref[...], LN_EPS)
    o_ref[...] = (y * (1.0 + sc_ref[...]) + sh_ref[...]).astype(o_ref.dtype)


def _lnmod(x, scale, shift):
    m, d = x.shape
    tm = _tile(m, 256)
    return pl.pallas_call(
        _lnmod_kernel,
        grid=(m // tm,),
        in_specs=[pl.BlockSpec((tm, d), lambda i: (i, 0)),
                  pl.BlockSpec((1, d), lambda i: (0, 0)),
                  pl.BlockSpec((1, d), lambda i: (0, 0))],
        out_specs=pl.BlockSpec((tm, d), lambda i: (i, 0)),
        out_shape=jax.ShapeDtypeStruct((m, d), BF16),
        compiler_params=_params(("arbitrary",)),
        name="lnmod",
    )(x, scale, shift)


def _rope_tile(z, cosf, sins):
    lane = lax.broadcasted_iota(jnp.int32, cosf.shape, 1)
    even = (lane % 2) == 0
    outs = []
    for c0 in range(0, z.shape[1], HEAD_DIM):
        zc = z[:, c0:c0 + HEAD_DIM]
        sw = jnp.where(even, pltpu.roll(zc, HEAD_DIM - 1, 1), pltpu.roll(zc, 1, 1))
        outs.append(zc * cosf + sw * sins)
    return outs[0] if len(outs) == 1 else jnp.concatenate(outs, -1)


def _inproj_kernel(a_ref, w_ref, b_ref, *rest, tn, seg, qscale, rope):
    if rope:
        cos_ref, sin_ref, o_ref = rest
    else:
        (o_ref,) = rest
    acc = jnp.dot(a_ref[...], w_ref[...], preferred_element_type=F32) + b_ref[...]
    col = pl.program_id(1) * tn
    in_naq = (col >= seg["na_q"][0]) & (col < seg["na_q"][1])
    in_dfq = (col >= seg["df_q"][0]) & (col < seg["df_q"][1])
    in_dfk = (col >= seg["df_k"][0]) & (col < seg["df_k"][1])

    def rot(z):
        return _rope_tile(z, cos_ref[...], sin_ref[...]) if rope else z

    @pl.when(in_naq)
    def _():
        o_ref[...] = (acc * qscale).astype(o_ref.dtype)

    @pl.when(in_dfq)
    def _():
        o_ref[...] = (rot(acc) * qscale).astype(o_ref.dtype)

    @pl.when(in_dfk)
    def _():
        o_ref[...] = rot(acc).astype(o_ref.dtype)

    @pl.when(jnp.logical_not(in_naq | in_dfq | in_dfk))
    def _():
        o_ref[...] = acc.astype(o_ref.dtype)


def _inproj(a, w, b, seg, rope_tabs, tm_pref, tn_pref):
    m, k = a.shape
    n = w.shape[1]
    tm = _tile(m, tm_pref)
    tn = _tile(math.gcd(*[s[1] - s[0] for s in seg.values()]), tn_pref)
    rope = rope_tabs is not None
    in_specs = [pl.BlockSpec((tm, k), lambda i, j: (i, 0)),
                pl.BlockSpec((k, tn), lambda i, j: (0, j)),
                pl.BlockSpec((1, tn), lambda i, j: (0, j))]
    args = [a, w, b.reshape(1, n)]
    if rope:
        in_specs += [pl.BlockSpec((tm, HEAD_DIM), lambda i, j: (i, 0))] * 2
        args += list(rope_tabs)
    return pl.pallas_call(
        functools.partial(_inproj_kernel, tn=tn, seg=seg, qscale=HEAD_DIM ** -0.5, rope=rope),
        grid=(m // tm, n // tn),
        in_specs=in_specs,
        out_specs=pl.BlockSpec((tm, tn), lambda i, j: (i, j)),
        out_shape=jax.ShapeDtypeStruct((m, n), BF16),
        compiler_params=_params(("arbitrary", "arbitrary")),
        name="inproj",
    )(*args)


def _mm_kernel(a_ref, w_ref, o_ref):
    o_ref[...] = jnp.dot(a_ref[...], w_ref[...], preferred_element_type=F32).astype(o_ref.dtype)


def _matmul(a, w, out_dtype, tm_pref, tn_pref):
    m, k = a.shape
    n = w.shape[1]
    tm, tn = _tile(m, tm_pref), _tile(n, tn_pref)
    return pl.pallas_call(
        _mm_kernel,
        grid=(m // tm, n // tn),
        in_specs=[pl.BlockSpec((tm, k), lambda i, j: (i, 0)),
                  pl.BlockSpec((k, tn), lambda i, j: (0, j))],
        out_specs=pl.BlockSpec((tm, tn), lambda i, j: (i, j)),
        out_shape=jax.ShapeDtypeStruct((m, n), out_dtype),
        compiler_params=_params(("arbitrary", "arbitrary")),
        name="outproj",
    )(a, w)


def _na_kernel(q_ref, *rest, kr, heads):
    k_refs, v_refs = rest[:kr], rest[kr:2 * kr]
    gate_ref, kc_ref, vc_ref, bias_ref, o_ref, ks_ref, vs_ref = rest[2 * kr:]
    w = q_ref.shape[1]
    for t in range(kr):
        ks_ref[t * w:(t + 1) * w, :] = k_refs[t][0]
        vs_ref[t * w:(t + 1) * w, :] = v_refs[t][0]
    for h in range(heads):
        sl = slice(h * HEAD_DIM, (h + 1) * HEAD_DIM)
        q = q_ref[0, :, sl]
        s = _dot_nt(q, ks_ref[:, sl]) + bias_ref[0, h]
        sc = _dot_nt(q, kc_ref[:, sl])
        m = jnp.maximum(jnp.max(s, -1, keepdims=True), jnp.max(sc, -1, keepdims=True))
        p = jnp.exp(s - m)
        pc = jnp.exp(sc - m)
        l = jnp.sum(p, -1, keepdims=True) + jnp.sum(pc, -1, keepdims=True)
        o = (jnp.dot(p.astype(BF16), vs_ref[:, sl], preferred_element_type=F32)
             + jnp.dot(pc.astype(BF16), vc_ref[:, sl], preferred_element_type=F32))
        g = gate_ref[0, :, sl].astype(F32)
        o_ref[0, :, sl] = (o / l * _silu(g)).astype(o_ref.dtype)


def _na_bias_table(rpb, kr):
    h = rpb.shape[0]
    d = jnp.arange(kr)[:, None]
    t = jnp.arange(kr)[None, :]
    row_off = t - d + (NA_KR - 1)
    cols = jnp.arange(GRID_W)
    col_start = jnp.clip(cols - NA_KW // 2, 0, GRID_W - NA_KW)
    kc = jnp.arange(GRID_W)[None, :]
    valid = (kc >= col_start[:, None]) & (kc < col_start[:, None] + NA_KW)
    col_off = jnp.clip(kc - cols[:, None] + (NA_KW - 1), 0, 2 * NA_KW - 2)
    b = rpb.astype(F32)[:, row_off[:, :, None, None], col_off[None, None, :, :]]
    b = jnp.where(valid[None, None, None], b, NEG_BIG)
    return jnp.transpose(b, (1, 0, 3, 2, 4)).reshape(kr, h, GRID_W, kr * GRID_W)


def _na(p, pc, bias, seg, out_rows):
    l, n_in = p.shape
    c = pc.shape[0]
    naw = seg["na_q"][1] - seg["na_q"][0]
    heads = naw // HEAD_DIM
    rows = l // GRID_W
    kr = bias.shape[0]
    w = GRID_W
    p3 = p.reshape(rows, w, n_in)
    cb = {name: seg[name][0] // naw for name in ("na_q", "na_k", "na_v", "na_gate")}

    def r0(r):
        return jnp.clip(r - kr // 2, 0, rows - kr)

    def row_spec(name, t):
        return pl.BlockSpec((1, w, naw), lambda r: (r0(r) + t, 0, cb[name]))

    in_specs = ([pl.BlockSpec((1, w, naw), lambda r: (r, 0, cb["na_q"]))]
                + [row_spec("na_k", t) for t in range(kr)]
                + [row_spec("na_v", t) for t in range(kr)]
                + [pl.BlockSpec((1, w, naw), lambda r: (r, 0, cb["na_gate"])),
                   pl.BlockSpec((c, naw), lambda r: (0, cb["na_k"])),
                   pl.BlockSpec((c, naw), lambda r: (0, cb["na_v"])),
                   pl.BlockSpec((1, heads, w, kr * w), lambda r: (r - r0(r), 0, 0, 0))])
    out = pl.pallas_call(
        functools.partial(_na_kernel, kr=kr, heads=heads),
        grid=(rows,),
        in_specs=in_specs,
        out_specs=pl.BlockSpec((1, w, naw), lambda r: (r, 0, 0)),
        out_shape=jax.ShapeDtypeStruct((rows, w, naw), BF16),
        scratch_shapes=[pltpu.VMEM((kr * w, naw), BF16), pltpu.VMEM((kr * w, naw), BF16)],
        compiler_params=_params(("arbitrary",)),
        name="natten",
    )(p3, *([p3] * (2 * kr)), p3, pc, pc, bias)
    return out.reshape(l, naw)


def _dense_kernel(q_ref, k_ref, v_ref, gate_ref, o_ref):
    s = _dot_nt(q_ref[...], k_ref[...])
    p = jnp.exp(s - jnp.max(s, -1, keepdims=True))
    l = jnp.sum(p, -1, keepdims=True)
    o = jnp.dot(p.astype(BF16), v_ref[...], preferred_element_type=F32)
    g = gate_ref[...].astype(F32)
    o_ref[...] = (o / l * _silu(g)).astype(o_ref.dtype)


def _dense_attn(pc, seg):
    c = pc.shape[0]
    naw = seg["na_q"][1] - seg["na_q"][0]
    heads = naw // HEAD_DIM
    cb = {name: seg[name][0] // HEAD_DIM for name in ("na_q", "na_k", "na_v", "na_gate")}

    def spec(name):
        return pl.BlockSpec((c, HEAD_DIM), lambda h: (0, cb[name] + h))

    return pl.pallas_call(
        _dense_kernel,
        grid=(heads,),
        in_specs=[spec("na_q"), spec("na_k"), spec("na_v"), spec("na_gate")],
        out_specs=pl.BlockSpec((c, HEAD_DIM), lambda h: (0, h)),
        out_shape=jax.ShapeDtypeStruct((c, naw), BF16),
        compiler_params=_params(("arbitrary",)),
        name="ctx_dense_attn",
    )(pc, pc, pc, pc)


def _diff_kernel(q_ref, k_ref, v_ref, *rest, nk, lam_init, has_ctx):
    if has_ctx:
        kc_ref, vc_ref = rest[:2]
        rest = rest[2:]
    gate_ref, lq1_ref, lk1_ref, lq2_ref, lk2_ref, g_ref, o_ref, m_ref, l_ref, acc_ref = rest
    ik = pl.program_id(2)

    @pl.when(ik == 0)
    def _():
        m_ref[...] = jnp.full(m_ref.shape, NEG_BIG, F32)
        l_ref[...] = jnp.zeros(l_ref.shape, F32)
        acc_ref[...] = jnp.zeros(acc_ref.shape, F32)

    def update(k_r, v_r):
        v = v_r[...]
        for i in range(2):
            sl = slice(i * HEAD_DIM, (i + 1) * HEAD_DIM)
            s = _dot_nt(q_ref[:, sl], k_r[:, sl])
            m_old = m_ref[i]
            m_new = jnp.maximum(m_old, jnp.max(s, -1, keepdims=True))
            alpha = jnp.exp(m_old - m_new)
            p = jnp.exp(s - m_new)
            l_ref[i] = alpha * l_ref[i] + jnp.sum(p, -1, keepdims=True)
            acc_ref[i] = alpha * acc_ref[i] + jnp.dot(p.astype(BF16), v, preferred_element_type=F32)
            m_ref[i] = m_new

    if has_ctx:
        @pl.when(ik == 0)
        def _():
            update(kc_ref, vc_ref)

    update(k_ref, v_ref)

    @pl.when(ik == nk - 1)
    def _():
        lam = (jnp.exp(jnp.sum(lq1_ref[...] * lk1_ref[...], -1, keepdims=True))
               - jnp.exp(jnp.sum(lq2_ref[...] * lk2_ref[...], -1, keepdims=True)) + lam_init)
        o = acc_ref[0] / l_ref[0] - lam * (acc_ref[1] / l_ref[1])
        o = o * lax.rsqrt(jnp.mean(o * o, -1, keepdims=True) + DIFF_LN_EPS) * g_ref[...]
        g = gate_ref[...].astype(F32)
        o_ref[...] = (o * (1.0 - lam_init) * _silu(g)).astype(o_ref.dtype)


def _diff_attn(pq, pk, pctx, lam_params, subln_g, seg, lam_init, tq_pref, tk_pref):
    lq, lk = pq.shape[0], pk.shape[0]
    hw = 2 * HEAD_DIM
    heads = (seg["df_v"][1] - seg["df_v"][0]) // hw
    cb = {name: seg[name][0] // hw for name in ("df_q", "df_k", "df_v", "df_gate")}
    tq, tk = _tile(lq, tq_pref), _tile(lk, tk_pref)
    nq, nk = lq // tq, lk // tk
    has_ctx = pctx is not None
    in_specs = [pl.BlockSpec((tq, hw), lambda h, i, j: (i, cb["df_q"] + h)),
                pl.BlockSpec((tk, hw), lambda h, i, j: (j, cb["df_k"] + h)),
                pl.BlockSpec((tk, hw), lambda h, i, j: (j, cb["df_v"] + h))]
    args = [pq, pk, pk]
    if has_ctx:
        c = pctx.shape[0]
        in_specs += [pl.BlockSpec((c, hw), lambda h, i, j: (0, cb["df_k"] + h)),
                     pl.BlockSpec((c, hw), lambda h, i, j: (0, cb["df_v"] + h))]
        args += [pctx, pctx]
    in_specs += [pl.BlockSpec((tq, hw), lambda h, i, j: (i, cb["df_gate"] + h))]
    in_specs += [pl.BlockSpec((1, HEAD_DIM), lambda h, i, j: (0, 0))] * 4
    in_specs += [pl.BlockSpec((1, hw), lambda h, i, j: (0, 0))]
    args += [pq] + [v.reshape(1, HEAD_DIM).astype(F32) for v in lam_params] + [subln_g.reshape(1, hw).astype(F32)]
    return pl.pallas_call(
        functools.partial(_diff_kernel, nk=nk, lam_init=lam_init, has_ctx=has_ctx),
        grid=(heads, nq, nk),
        in_specs=in_specs,
        out_specs=pl.BlockSpec((tq, hw), lambda h, i, j: (i, h)),
        out_shape=jax.ShapeDtypeStruct((lq, heads * hw), BF16),
        scratch_shapes=[pltpu.VMEM((2, tq, 1), F32), pltpu.VMEM((2, tq, 1), F32),
                        pltpu.VMEM((2, tq, hw), F32)],
        compiler_params=_params(("arbitrary", "arbitrary", "arbitrary")),
        name="diff_attn",
    )(*args)


def _conv_kernel(vp_ref, vc_ref, vn_ref, gp_ref, gc_ref, gn_ref, gate_ref, w_ref, b_ref, lg_ref, lb_ref,
                 o_ref, u_ref, y_ref, *, ts, nt, rb):
    i = pl.program_id(0)
    nch = u_ref.shape[0]

    def glu(v_r, g_r, c0):
        v = v_r[:, c0:c0 + HEAD_DIM].astype(F32)
        g = g_r[:, c0:c0 + HEAD_DIM].astype(F32)
        return v * jax.nn.sigmoid(g)

    has_prev = (i > 0).astype(F32)
    has_next = (i < nt - 1).astype(F32)
    for c in range(nch):
        c0 = c * HEAD_DIM
        u_ref[c, 0:CONV_HALO, :] = glu(vp_ref, gp_ref, c0) * has_prev
        u_ref[c, CONV_HALO:CONV_HALO + ts, :] = glu(vc_ref, gc_ref, c0)
        u_ref[c, CONV_HALO + ts:2 * CONV_HALO + ts, :] = glu(vn_ref, gn_ref, c0) * has_next

    base = CONV_HALO - CONV_K // 2

    def chunk(c, carry):
        for r0 in range(0, ts, rb):
            acc = jnp.zeros((rb, HEAD_DIM), F32)
            for j in range(CONV_K):
                acc = acc + w_ref[c, j:j + 1, :] * u_ref[c, pl.ds(base + r0 + j, rb), :]
            y_ref[c, r0:r0 + rb, :] = acc
        return carry

    lax.fori_loop(0, nch, chunk, 0)

    cw = nch * HEAD_DIM
    y = jnp.concatenate([y_ref[c] for c in range(nch)], -1) + b_ref[...]
    y = _ln_rows(y, CONV_LN_EPS) * lg_ref[...] + lb_ref[...]
    g = gate_ref[...].astype(F32)
    o_ref[...] = (_silu(y) * _silu(g)).astype(o_ref.dtype)
    del cw


def _conv(p, conv_w, conv_b, ln_g, ln_b, seg, ts_pref):
    l = p.shape[0]
    cw = seg["cv_val"][1] - seg["cv_val"][0]
    nch = cw // HEAD_DIM
    ts = _tile(l, ts_pref)
    nt = l // ts
    hb = ts // CONV_HALO
    nhb = l // CONV_HALO
    cb = {name: seg[name][0] // cw for name in ("cv_val", "cv_glu", "cv_gate")}

    def specs(name):
        return [pl.BlockSpec((CONV_HALO, cw), lambda i: (jnp.maximum(i * hb - 1, 0), cb[name])),
                pl.BlockSpec((ts, cw), lambda i: (i, cb[name])),
                pl.BlockSpec((CONV_HALO, cw), lambda i: (jnp.minimum((i + 1) * hb, nhb - 1), cb[name]))]

    vec = pl.BlockSpec((1, cw), lambda i: (0, 0))
    w3 = jnp.transpose(conv_w.astype(F32).reshape(CONV_K, nch, HEAD_DIM), (1, 0, 2))
    return pl.pallas_call(
        functools.partial(_conv_kernel, ts=ts, nt=nt, rb=_tile(ts, 64)),
        grid=(nt,),
        in_specs=specs("cv_val") + specs("cv_glu") + [pl.BlockSpec((ts, cw), lambda i: (i, cb["cv_gate"])),
                                                      pl.BlockSpec((nch, CONV_K, HEAD_DIM), lambda i: (0, 0, 0)),
                                                      vec, vec, vec],
        out_specs=pl.BlockSpec((ts, cw), lambda i: (i, 0)),
        out_shape=jax.ShapeDtypeStruct((l, cw), BF16),
        scratch_shapes=[pltpu.VMEM((nch, ts + 2 * CONV_HALO, HEAD_DIM), F32),
                        pltpu.VMEM((nch, ts, HEAD_DIM), F32)],
        compiler_params=_params(("arbitrary",)),
        name="conformer_conv",
    )(p, p, p, p, p, p, p, w3, conv_b.reshape(1, cw).astype(F32), ln_g.reshape(1, cw).astype(F32),
      ln_b.reshape(1, cw).astype(F32))


def _merge_kernel(a1_ref, a2_ref, a3_ref, w1_ref, w2_ref, w3_ref, m1_ref, m2_ref, m3_ref, o_ref):
    def branch(a_ref, w_ref, m_ref):
        y = jnp.dot(a_ref[...], w_ref[...], preferred_element_type=F32)
        return jax.nn.sigmoid(m_ref[...].astype(F32)) * y

    o_ref[...] = (branch(a1_ref, w1_ref, m1_ref) + branch(a2_ref, w2_ref, m2_ref)
                  + branch(a3_ref, w3_ref, m3_ref)).astype(o_ref.dtype)


def _merge(a_na, a_df, a_cv, w_na, w_df, w_cv, p, seg, tm_pref, tn_pref):
    m = a_na.shape[0]
    d = w_na.shape[1]
    tm, tn = _tile(m, tm_pref), _tile(d, tn_pref)
    cb = {name: seg[name][0] // tn for name in ("merge_na", "merge_df", "merge_cv")}

    def a_spec(a):
        return pl.BlockSpec((tm, a.shape[1]), lambda i, j: (i, 0))

    def w_spec(w):
        return pl.BlockSpec((w.shape[0], tn), lambda i, j: (0, j))

    def m_spec(name):
        return pl.BlockSpec((tm, tn), lambda i, j: (i, cb[name] + j))

    return pl.pallas_call(
        _merge_kernel,
        grid=(m // tm, d // tn),
        in_specs=[a_spec(a_na), a_spec(a_df), a_spec(a_cv), w_spec(w_na), w_spec(w_df), w_spec(w_cv),
                  m_spec("merge_na"), m_spec("merge_df"), m_spec("merge_cv")],
        out_specs=pl.BlockSpec((tm, tn), lambda i, j: (i, j)),
        out_shape=jax.ShapeDtypeStruct((m, d), BF16),
        compiler_params=_params(("arbitrary", "arbitrary")),
        name="merge",
    )(a_na, a_df, a_cv, w_na, w_df, w_cv, p, p, p)


def _post_kernel(x_ref, z_ref, gate_ref, g_ref, b_ref, *rest, alpha, emit_h):
    if emit_h:
        sc_ref, sh_ref, o_ref, h_ref = rest
    else:
        (o_ref,) = rest
    v = alpha * x_ref[...] + gate_ref[...] * z_ref[...]
    y = _ln_rows(v, LN_EPS) * g_ref[...] + b_ref[...]
    o_ref[...] = y
    if emit_h:
        h_ref[...] = (_ln_rows(y, LN_EPS) * (1.0 + sc_ref[...]) + sh_ref[...]).astype(h_ref.dtype)


def _post(x, z, gate, g, b, alpha, next_mod):
    m, d = x.shape
    tm = _tile(m, 256)
    emit_h = next_mod is not None
    row = pl.BlockSpec((tm, d), lambda i: (i, 0))
    vec = pl.BlockSpec((1, d), lambda i: (0, 0))
    in_specs = [row, row, vec, vec, vec]
    args = [x, z, gate, g.reshape(1, d).astype(F32), b.reshape(1, d).astype(F32)]
    out_specs, out_shape = row, jax.ShapeDtypeStruct((m, d), F32)
    if emit_h:
        in_specs += [vec, vec]
        args += list(next_mod)
        out_specs = [row, row]
        out_shape = [out_shape, jax.ShapeDtypeStruct((m, d), BF16)]
    return pl.pallas_call(
        functools.partial(_post_kernel, alpha=alpha, emit_h=emit_h),
        grid=(m // tm,),
        in_specs=in_specs,
        out_specs=out_specs,
        out_shape=out_shape,
        compiler_params=_params(("arbitrary",)),
        name="post_ln",
    )(*args)


def _rope_tables(n_tokens):
    t = jnp.arange(n_tokens, dtype=jnp.int32)
    row = (t // GRID_W).astype(F32)
    col = (t % GRID_W).astype(F32)
    n_pairs_axis = HEAD_DIM // 4
    inv_freq = ROPE_BASE ** (-jnp.arange(n_pairs_axis, dtype=F32) / n_pairs_axis)
    ang = jnp.concatenate([row[:, None] * inv_freq, col[:, None] * inv_freq], -1)
    cos, sin = jnp.cos(ang), jnp.sin(ang)
    cosf = jnp.repeat(cos, 2, axis=-1)
    sins = jnp.stack([-sin, sin], -1).reshape(n_tokens, HEAD_DIM)
    return cosf, sins


def _segments(d, naw, dqw, dw, cw):
    sizes = (("na_q", naw), ("na_k", naw), ("na_v", naw), ("na_gate", naw),
             ("df_q", dqw), ("df_k", dqw), ("df_v", dw), ("df_gate", dw),
             ("cv_val", cw), ("cv_glu", cw), ("cv_gate", cw),
             ("merge_na", d), ("merge_df", d), ("merge_cv", d))
    seg, off = {}, 0
    for name, size in sizes:
        seg[name] = (off, off + size)
        off += size
    return seg, off


def kernel(x, c, ctx, c_ctx, w_ada, b_ada, w_in, b_in, na_rpb, diff_lq1, diff_lk1, diff_lq2, diff_lk2, diff_subln_g, conv_w, conv_b, conv_ln_g, conv_ln_b, w_proj_na, w_proj_diff, w_proj_conv, w_out, post_ln_g, post_ln_b):
    b, l, d = x.shape
    assert b == 1 and c.shape[0] == 1 and ctx.shape[0] == 1
    depth = w_ada.shape[0]
    naw, dw, cw = w_proj_na.shape[1], w_proj_diff.shape[1], w_proj_conv.shape[1]
    n_in = w_in.shape[-1]
    dqw = (n_in - 4 * naw - 2 * dw - 3 * cw - 3 * d) // 2
    seg, total = _segments(d, naw, dqw, dw, cw)
    assert total == n_in and dqw == dw
    rows = l // GRID_W
    kr = min(NA_KR, rows)
    alpha = (2.0 * depth) ** 0.25

    xl = x[0]
    xc = ctx[0]
    cc = jnp.zeros((8, d), F32).at[0].set(c[0]).at[1].set(c_ctx)
    mod = _ada(cc, w_ada, b_ada)

    def mod_vecs(i, r):
        m = mod[i, r]
        return m[None, :d], m[None, d:2 * d], m[None, 2 * d:]

    rope_tabs = _rope_tables(l)
    shift, scale, _ = mod_vecs(0, 0)
    h = _lnmod(xl, scale, shift)
    for i in range(depth):
        last = i == depth - 1
        lam_init = 0.8 - 0.6 * math.exp(-0.3 * i)
        _, _, gate = mod_vecs(i, 0)
        shift_c, scale_c, gate_c = mod_vecs(i, 1)
        w_in_i = w_in[i].astype(BF16)
        w_na, w_df, w_cv = w_proj_na[i].astype(BF16), w_proj_diff[i].astype(BF16), w_proj_conv[i].astype(BF16)
        w_o = w_out[i].astype(BF16)
        lam_params = (diff_lq1[i], diff_lk1[i], diff_lq2[i], diff_lk2[i])

        hc = _lnmod(xc, scale_c, shift_c)
        p = _inproj(h, w_in_i, b_in[i], seg, rope_tabs, 1024, 512)
        pc = _inproj(hc, w_in_i, b_in[i], seg, None, 256, 512)

        a_na = _na(p, pc, _na_bias_table(na_rpb[i], kr), seg, rows)
        a_df = _diff_attn(p, p, pc, lam_params, diff_subln_g[i], seg, lam_init, 512, 512)
        a_cv = _conv(p, conv_w[i], conv_b[i], conv_ln_g[i], conv_ln_b[i], seg, 256)
        ym = _merge(a_na, a_df, a_cv, w_na, w_df, w_cv, p, seg, 512, 512)
        z = _matmul(ym, w_o, F32, 1024, 512)

        if not last:
            ac_na = _dense_attn(pc, seg)
            ac_df = _diff_attn(pc, pc, None, lam_params, diff_subln_g[i], seg, lam_init, 256, 256)
            ac_cv = _conv(pc, conv_w[i], conv_b[i], conv_ln_g[i], conv_ln_b[i], seg, 256)
            ymc = _merge(ac_na, ac_df, ac_cv, w_na, w_df, w_cv, pc, seg, 256, 512)
            zc = _matmul(ymc, w_o, F32, 256, 512)
            xc = _post(xc, zc, gate_c, post_ln_g[i], post_ln_b[i], alpha, None)
            shift_n, scale_n, _ = mod_vecs(i + 1, 0)
            xl, h = _post(xl, z, gate, post_ln_g[i], post_ln_b[i], alpha, (scale_n, shift_n))
        else:
            xl = _post(xl, z, gate, post_ln_g[i], post_ln_b[i], alpha, None)
    return xl[None]
```

```python
import functools
import math

import jax
import jax.numpy as jnp
from jax import lax
from jax.experimental import pallas as pl
from jax.experimental.pallas import tpu as pltpu

HEAD_DIM = 128
GRID_W = 64
NA_KR = 8
NA_KW = 16
CONV_K = 31
CONV_HALO = 16
ROPE_BASE = 10000.0
LN_EPS = 1e-6
DIFF_LN_EPS = 1e-5
CONV_LN_EPS = 1e-5
NEG_BIG = -1e30
LOG2E = math.log2(math.e)
QSCALE = HEAD_DIM ** -0.5 * LOG2E
V7X_VMEM_LIMIT = 56 * 1024 * 1024

F32 = jnp.float32
BF16 = jnp.bfloat16


def _tile(n, pref):
    t = min(n, pref)
    while n % t:
        t //= 2
    return t


def _params(sem):
    return pltpu.CompilerParams(dimension_semantics=sem, vmem_limit_bytes=V7X_VMEM_LIMIT)


def _silu(x):
    return x * jax.nn.sigmoid(x)


def _ln_rows(x, eps):
    mu = jnp.mean(x, -1, keepdims=True)
    xc = x - mu
    var = jnp.mean(xc * xc, -1, keepdims=True)
    return xc * lax.rsqrt(var + eps)


def _dot_nt(a, b):
    return lax.dot_general(a, b, (((1,), (1,)), ((), ())), preferred_element_type=F32)


def _ada_kernel(cc_ref, w_ref, b_ref, o_ref):
    s = _silu(cc_ref[...])
    o_ref[0] = jnp.dot(s.astype(BF16), w_ref[0].astype(BF16), preferred_element_type=F32) + b_ref[0]


def _ada(cc, w_ada, b_ada):
    depth, d, n = w_ada.shape
    tn = _tile(n, 512)
    return pl.pallas_call(
        _ada_kernel,
        grid=(depth, n // tn),
        in_specs=[pl.BlockSpec((8, d), lambda i, j: (0, 0)),
                  pl.BlockSpec((1, d, tn), lambda i, j: (i, 0, j)),
                  pl.BlockSpec((1, 1, tn), lambda i, j: (i, 0, j))],
        out_specs=pl.BlockSpec((1, 8, tn), lambda i, j: (i, 0, j)),
        out_shape=jax.ShapeDtypeStruct((depth, 8, n), F32),
        compiler_params=_params(("arbitrary", "arbitrary")),
        name="ada",
    )(cc, w_ada, b_ada.reshape(depth, 1, n))


def _lnmod_kernel(x_ref, sc_ref, sh_ref, o_ref):
    y = _ln_rows(x_ref[...], LN_EPS)
    o_ref[...] = (y * (1.0 + sc_ref[...]) + sh_ref[...]).astype(o_ref.dtype)


def _lnmod(x, scale, shift):
    m, d = x.shape
    tm = _tile(m, 256)
    return pl.pallas_call(
        _lnmod_kernel,
        grid=(m // tm,),
        in_specs=[pl.BlockSpec((tm, d), lambda i: (i, 0)),
                  pl.BlockSpec((1, d), lambda i: (0, 0)),
                  pl.BlockSpec((1, d), lambda i: (0, 0))],
        out_specs=pl.BlockSpec((tm, d), lambda i: (i, 0)),
        out_shape=jax.ShapeDtypeStruct((m, d), BF16),
        compiler_params=_params(("arbitrary",)),
        name="lnmod",
    )(x, scale, shift)


def _rope_tile(z, cosf, sins):
    lane = lax.broadcasted_iota(jnp.int32, cosf.shape, 1)
    even = (lane % 2) == 0
    outs = []
    for c0 in range(0, z.shape[1], HEAD_DIM):
        zc = z[:, c0:c0 + HEAD_DIM]
        sw = jnp.where(even, pltpu.roll(zc, HEAD_DIM - 1, 1), pltpu.roll(zc, 1, 1))
        outs.append(zc * cosf + sw * sins)
    return outs[0] if len(outs) == 1 else jnp.concatenate(outs, -1)


def _inproj_kernel(a_ref, w_ref, b_ref, *rest, tn, seg, qscale, rope):
    if rope:
        cos_ref, sin_ref, o_ref = rest
    else:
        (o_ref,) = rest
    acc = jnp.dot(a_ref[...], w_ref[...], preferred_element_type=F32) + b_ref[...]
    col = pl.program_id(1) * tn
    in_naq = (col >= seg["na_q"][0]) & (col < seg["na_q"][1])
    in_dfq = (col >= seg["df_q"][0]) & (col < seg["df_q"][1])
    in_dfk = (col >= seg["df_k"][0]) & (col < seg["df_k"][1])

    def rot(z):
        return _rope_tile(z, cos_ref[...], sin_ref[...]) if rope else z

    @pl.when(in_naq)
    def _():
        o_ref[...] = (acc * qscale).astype(o_ref.dtype)

    @pl.when(in_dfq)
    def _():
        o_ref[...] = (rot(acc) * qscale).astype(o_ref.dtype)

    @pl.when(in_dfk)
    def _():
        o_ref[...] = rot(acc).astype(o_ref.dtype)

    @pl.when(jnp.logical_not(in_naq | in_dfq | in_dfk))
    def _():
        o_ref[...] = acc.astype(o_ref.dtype)


def _inproj(a, w, b, seg, rope_tabs, tm_pref, tn_pref):
    m, k = a.shape
    n = w.shape[1]
    tm = _tile(m, tm_pref)
    tn = _tile(math.gcd(*[s[1] - s[0] for s in seg.values()]), tn_pref)
    rope = rope_tabs is not None
    in_specs = [pl.BlockSpec((tm, k), lambda i, j: (i, 0)),
                pl.BlockSpec((k, tn), lambda i, j: (0, j)),
                pl.BlockSpec((1, tn), lambda i, j: (0, j))]
    args = [a, w, b.reshape(1, n)]
    if rope:
        in_specs += [pl.BlockSpec((tm, HEAD_DIM), lambda i, j: (i, 0))] * 2
        args += list(rope_tabs)
    return pl.pallas_call(
        functools.partial(_inproj_kernel, tn=tn, seg=seg, qscale=QSCALE, rope=rope),
        grid=(m // tm, n // tn),
        in_specs=in_specs,
        out_specs=pl.BlockSpec((tm, tn), lambda i, j: (i, j)),
        out_shape=jax.ShapeDtypeStruct((m, n), BF16),
        compiler_params=_params(("arbitrary", "arbitrary")),
        name="inproj",
    )(*args)


def _mm_kernel(a_ref, w_ref, o_ref):
    o_ref[...] = jnp.dot(a_ref[...], w_ref[...], preferred_element_type=F32).astype(o_ref.dtype)


def _matmul(a, w, out_dtype, tm_pref, tn_pref):
    m, k = a.shape
    n = w.shape[1]
    tm, tn = _tile(m, tm_pref), _tile(n, tn_pref)
    return pl.pallas_call(
        _mm_kernel,
        grid=(m // tm, n // tn),
        in_specs=[pl.BlockSpec((tm, k), lambda i, j: (i, 0)),
                  pl.BlockSpec((k, tn), lambda i, j: (0, j))],
        out_specs=pl.BlockSpec((tm, tn), lambda i, j: (i, j)),
        out_shape=jax.ShapeDtypeStruct((m, n), out_dtype),
        compiler_params=_params(("arbitrary", "arbitrary")),
        name="outproj",
    )(a, w)


def _na_kernel(q_ref, *rest, kr, heads):
    k_refs, v_refs = rest[:kr], rest[kr:2 * kr]
    gate_ref, kc_ref, vc_ref, bias_ref, o_ref, ks_ref, vs_ref = rest[2 * kr:]
    w = q_ref.shape[1]
    for t in range(kr):
        ks_ref[t * w:(t + 1) * w, :] = k_refs[t][0]
        vs_ref[t * w:(t + 1) * w, :] = v_refs[t][0]
    for h in range(heads):
        sl = slice(h * HEAD_DIM, (h + 1) * HEAD_DIM)
        q = q_ref[0, :, sl]
        s = _dot_nt(q, ks_ref[:, sl]) + bias_ref[0, h]
        sc = _dot_nt(q, kc_ref[:, sl])
        m = jnp.maximum(jnp.max(s, -1, keepdims=True), jnp.max(sc, -1, keepdims=True))
        p = jnp.exp2(s - m)
        pc = jnp.exp2(sc - m)
        l = jnp.sum(p, -1, keepdims=True) + jnp.sum(pc, -1, keepdims=True)
        o = (jnp.dot(p.astype(BF16), vs_ref[:, sl], preferred_element_type=F32)
             + jnp.dot(pc.astype(BF16), vc_ref[:, sl], preferred_element_type=F32))
        g = gate_ref[0, :, sl].astype(F32)
        o_ref[0, :, sl] = (o / l * _silu(g)).astype(o_ref.dtype)


def _na_bias_table(rpb, kr):
    h = rpb.shape[0]
    cols = jnp.arange(GRID_W)
    col_start = jnp.clip(cols - NA_KW // 2, 0, GRID_W - NA_KW)
    kc = cols[None, :]
    valid = (kc >= col_start[:, None]) & (kc < col_start[:, None] + NA_KW)
    col_off = kc - cols[:, None] + (NA_KW - 1)
    onehot = ((col_off[None] == jnp.arange(2 * NA_KW - 1)[:, None, None]) & valid[None]).astype(F32)
    tz = jnp.einsum("hrc,cjk->hrjk", rpb.astype(F32) * LOG2E, onehot, precision=lax.Precision.HIGHEST)
    tz = jnp.where(valid[None, None], tz, NEG_BIG)
    tabs = [tz[:, NA_KR - 1 - d:NA_KR - 1 - d + kr] for d in range(kr)]
    b = jnp.stack(tabs)
    return jnp.transpose(b, (0, 1, 3, 2, 4)).reshape(kr, h, GRID_W, kr * GRID_W)


def _na(p, pc, bias, seg, out_rows):
    l, n_in = p.shape
    c = pc.shape[0]
    naw = seg["na_q"][1] - seg["na_q"][0]
    heads = naw // HEAD_DIM
    rows = l // GRID_W
    kr = bias.shape[0]
    w = GRID_W
    p3 = p.reshape(rows, w, n_in)
    cb = {name: seg[name][0] // naw for name in ("na_q", "na_k", "na_v", "na_gate")}

    def r0(r):
        return jnp.clip(r - kr // 2, 0, rows - kr)

    def row_spec(name, t):
        return pl.BlockSpec((1, w, naw), lambda r: (r0(r) + t, 0, cb[name]))

    in_specs = ([pl.BlockSpec((1, w, naw), lambda r: (r, 0, cb["na_q"]))]
                + [row_spec("na_k", t) for t in range(kr)]
                + [row_spec("na_v", t) for t in range(kr)]
                + [pl.BlockSpec((1, w, naw), lambda r: (r, 0, cb["na_gate"])),
                   pl.BlockSpec((c, naw), lambda r: (0, cb["na_k"])),
                   pl.BlockSpec((c, naw), lambda r: (0, cb["na_v"])),
                   pl.BlockSpec((1, heads, w, kr * w), lambda r: (r - r0(r), 0, 0, 0))])
    out = pl.pallas_call(
        functools.partial(_na_kernel, kr=kr, heads=heads),
        grid=(rows,),
        in_specs=in_specs,
        out_specs=pl.BlockSpec((1, w, naw), lambda r: (r, 0, 0)),
        out_shape=jax.ShapeDtypeStruct((rows, w, naw), BF16),
        scratch_shapes=[pltpu.VMEM((kr * w, naw), BF16), pltpu.VMEM((kr * w, naw), BF16)],
        compiler_params=_params(("arbitrary",)),
        name="natten",
    )(p3, *([p3] * (2 * kr)), p3, pc, pc, bias)
    return out.reshape(l, naw)


def _dense_kernel(q_ref, k_ref, v_ref, gate_ref, o_ref):
    s = _dot_nt(q_ref[...], k_ref[...])
    p = jnp.exp2(s - jnp.max(s, -1, keepdims=True))
    l = jnp.sum(p, -1, keepdims=True)
    o = jnp.dot(p.astype(BF16), v_ref[...], preferred_element_type=F32)
    g = gate_ref[...].astype(F32)
    o_ref[...] = (o / l * _silu(g)).astype(o_ref.dtype)


def _dense_attn(pc, seg):
    c = pc.shape[0]
    naw = seg["na_q"][1] - seg["na_q"][0]
    heads = naw // HEAD_DIM
    cb = {name: seg[name][0] // HEAD_DIM for name in ("na_q", "na_k", "na_v", "na_gate")}

    def spec(name):
        return pl.BlockSpec((c, HEAD_DIM), lambda h: (0, cb[name] + h))

    return pl.pallas_call(
        _dense_kernel,
        grid=(heads,),
        in_specs=[spec("na_q"), spec("na_k"), spec("na_v"), spec("na_gate")],
        out_specs=pl.BlockSpec((c, HEAD_DIM), lambda h: (0, h)),
        out_shape=jax.ShapeDtypeStruct((c, naw), BF16),
        compiler_params=_params(("arbitrary",)),
        name="ctx_dense_attn",
    )(pc, pc, pc, pc)


def _diff_kernel(q_ref, k_ref, v_ref, *rest, nk, lam_init, has_ctx):
    if has_ctx:
        kc_ref, vc_ref = rest[:2]
        rest = rest[2:]
    gate_ref, lq1_ref, lk1_ref, lq2_ref, lk2_ref, g_ref, o_ref, m_ref, l_ref, acc_ref = rest
    ik = pl.program_id(2)

    @pl.when(ik == 0)
    def _():
        m_ref[...] = jnp.full(m_ref.shape, NEG_BIG, F32)
        l_ref[...] = jnp.zeros(l_ref.shape, F32)
        acc_ref[...] = jnp.zeros(acc_ref.shape, F32)

    def update(k_r, v_r):
        nch = k_r.shape[0] // HEAD_DIM
        for i in range(2):
            sl = slice(i * HEAD_DIM, (i + 1) * HEAD_DIM)
            s = _dot_nt(q_ref[:, sl], k_r[:, sl])
            chunks = [s[:, c * HEAD_DIM:(c + 1) * HEAD_DIM] for c in range(nch)]
            m_loc = functools.reduce(jnp.maximum, chunks)
            m_old = m_ref[i]
            m_new = jnp.maximum(m_old, jnp.max(m_loc, -1, keepdims=True))
            alpha = jnp.exp2(m_old - m_new)
            ps = [jnp.exp2(ch - m_new) for ch in chunks]
            l_ref[i] = alpha * l_ref[i] + functools.reduce(jnp.add, ps)
            p = jnp.concatenate([x.astype(BF16) for x in ps], -1)
            pv = jnp.dot(p, v_r[...], preferred_element_type=F32)
            acc_ref[i] = jnp.concatenate([alpha, alpha], -1) * acc_ref[i] + pv
            m_ref[i] = m_new

    if has_ctx:
        @pl.when(ik == 0)
        def _():
            update(kc_ref, vc_ref)

    update(k_ref, v_ref)

    @pl.when(ik == nk - 1)
    def _():
        lam = (jnp.exp(jnp.sum(lq1_ref[...] * lk1_ref[...], -1, keepdims=True))
               - jnp.exp(jnp.sum(lq2_ref[...] * lk2_ref[...], -1, keepdims=True)) + lam_init)
        l0 = jnp.sum(l_ref[0], -1, keepdims=True)
        l1 = jnp.sum(l_ref[1], -1, keepdims=True)
        o = acc_ref[0] / l0 - lam * (acc_ref[1] / l1)
        o = o * lax.rsqrt(jnp.mean(o * o, -1, keepdims=True) + DIFF_LN_EPS) * g_ref[...]
        g = gate_ref[...].astype(F32)
        o_ref[...] = (o * (1.0 - lam_init) * _silu(g)).astype(o_ref.dtype)


def _diff_attn(pq, pk, pctx, lam_params, subln_g, seg, lam_init, tq_pref, tk_pref):
    lq, lk = pq.shape[0], pk.shape[0]
    hw = 2 * HEAD_DIM
    heads = (seg["df_v"][1] - seg["df_v"][0]) // hw
    cb = {name: seg[name][0] // hw for name in ("df_q", "df_k", "df_v", "df_gate")}
    tq, tk = _tile(lq, tq_pref), _tile(lk, tk_pref)
    nq, nk = lq // tq, lk // tk
    has_ctx = pctx is not None
    in_specs = [pl.BlockSpec((tq, hw), lambda h, i, j: (i, cb["df_q"] + h)),
                pl.BlockSpec((tk, hw), lambda h, i, j: (j, cb["df_k"] + h)),
                pl.BlockSpec((tk, hw), lambda h, i, j: (j, cb["df_v"] + h))]
    args = [pq, pk, pk]
    if has_ctx:
        c = pctx.shape[0]
        in_specs += [pl.BlockSpec((c, hw), lambda h, i, j: (0, cb["df_k"] + h)),
                     pl.BlockSpec((c, hw), lambda h, i, j: (0, cb["df_v"] + h))]
        args += [pctx, pctx]
    in_specs += [pl.BlockSpec((tq, hw), lambda h, i, j: (i, cb["df_gate"] + h))]
    in_specs += [pl.BlockSpec((1, HEAD_DIM), lambda h, i, j: (0, 0))] * 4
    in_specs += [pl.BlockSpec((1, hw), lambda h, i, j: (0, 0))]
    args += [pq] + [v.reshape(1, HEAD_DIM).astype(F32) for v in lam_params] + [subln_g.reshape(1, hw).astype(F32)]
    return pl.pallas_call(
        functools.partial(_diff_kernel, nk=nk, lam_init=lam_init, has_ctx=has_ctx),
        grid=(heads, nq, nk),
        in_specs=in_specs,
        out_specs=pl.BlockSpec((tq, hw), lambda h, i, j: (i, h)),
        out_shape=jax.ShapeDtypeStruct((lq, heads * hw), BF16),
        scratch_shapes=[pltpu.VMEM((2, tq, HEAD_DIM), F32), pltpu.VMEM((2, tq, HEAD_DIM), F32),
                        pltpu.VMEM((2, tq, hw), F32)],
        compiler_params=_params(("arbitrary", "arbitrary", "arbitrary")),
        name="diff_attn",
    )(*args)


def _conv_kernel(vp_ref, vc_ref, vn_ref, gp_ref, gc_ref, gn_ref, gate_ref, w_ref, b_ref, lg_ref, lb_ref,
                 o_ref, u_ref, y_ref, *, ts, nt, rb):
    i = pl.program_id(0)
    nch = u_ref.shape[0]

    def glu(v_r, g_r, c0):
        v = v_r[:, c0:c0 + HEAD_DIM].astype(F32)
        g = g_r[:, c0:c0 + HEAD_DIM].astype(F32)
        return v * jax.nn.sigmoid(g)

    has_prev = (i > 0).astype(F32)
    has_next = (i < nt - 1).astype(F32)
    for c in range(nch):
        c0 = c * HEAD_DIM
        u_ref[c, 0:CONV_HALO, :] = glu(vp_ref, gp_ref, c0) * has_prev
        u_ref[c, CONV_HALO:CONV_HALO + ts, :] = glu(vc_ref, gc_ref, c0)
        u_ref[c, CONV_HALO + ts:2 * CONV_HALO + ts, :] = glu(vn_ref, gn_ref, c0) * has_next

    base = CONV_HALO - CONV_K // 2

    def chunk(c, carry):
        for r0 in range(0, ts, rb):
            acc = jnp.zeros((rb, HEAD_DIM), F32)
            for j in range(CONV_K):
                acc = acc + w_ref[c, j:j + 1, :] * u_ref[c, pl.ds(base + r0 + j, rb), :]
            y_ref[c, r0:r0 + rb, :] = acc
        return carry

    lax.fori_loop(0, nch, chunk, 0)

    cw = nch * HEAD_DIM
    y = jnp.concatenate([y_ref[c] for c in range(nch)], -1) + b_ref[...]
    y = _ln_rows(y, CONV_LN_EPS) * lg_ref[...] + lb_ref[...]
    g = gate_ref[...].astype(F32)
    o_ref[...] = (_silu(y) * _silu(g)).astype(o_ref.dtype)
    del cw


def _conv(p, conv_w, conv_b, ln_g, ln_b, seg, ts_pref):
    l = p.shape[0]
    cw = seg["cv_val"][1] - seg["cv_val"][0]
    nch = cw // HEAD_DIM
    ts = _tile(l, ts_pref)
    nt = l // ts
    hb = ts // CONV_HALO
    nhb = l // CONV_HALO
    cb = {name: seg[name][0] // cw for name in ("cv_val", "cv_glu", "cv_gate")}

    def specs(name):
        return [pl.BlockSpec((CONV_HALO, cw), lambda i: (jnp.maximum(i * hb - 1, 0), cb[name])),
                pl.BlockSpec((ts, cw), lambda i: (i, cb[name])),
                pl.BlockSpec((CONV_HALO, cw), lambda i: (jnp.minimum((i + 1) * hb, nhb - 1), cb[name]))]

    vec = pl.BlockSpec((1, cw), lambda i: (0, 0))
    w3 = jnp.transpose(conv_w.astype(F32).reshape(CONV_K, nch, HEAD_DIM), (1, 0, 2))
    return pl.pallas_call(
        functools.partial(_conv_kernel, ts=ts, nt=nt, rb=_tile(ts, 64)),
        grid=(nt,),
        in_specs=specs("cv_val") + specs("cv_glu") + [pl.BlockSpec((ts, cw), lambda i: (i, cb["cv_gate"])),
                                                      pl.BlockSpec((nch, CONV_K, HEAD_DIM), lambda i: (0, 0, 0)),
                                                      vec, vec, vec],
        out_specs=pl.BlockSpec((ts, cw), lambda i: (i, 0)),
        out_shape=jax.ShapeDtypeStruct((l, cw), BF16),
        scratch_shapes=[pltpu.VMEM((nch, ts + 2 * CONV_HALO, HEAD_DIM), F32),
                        pltpu.VMEM((nch, ts, HEAD_DIM), F32)],
        compiler_params=_params(("arbitrary",)),
        name="conformer_conv",
    )(p, p, p, p, p, p, p, w3, conv_b.reshape(1, cw).astype(F32), ln_g.reshape(1, cw).astype(F32),
      ln_b.reshape(1, cw).astype(F32))


def _merge_kernel(a1_ref, a2_ref, a3_ref, w1_ref, w2_ref, w3_ref, m1_ref, m2_ref, m3_ref, o_ref):
    def branch(a_ref, w_ref, m_ref):
        y = jnp.dot(a_ref[...], w_ref[...], preferred_element_type=F32)
        return jax.nn.sigmoid(m_ref[...].astype(F32)) * y

    o_ref[...] = (branch(a1_ref, w1_ref, m1_ref) + branch(a2_ref, w2_ref, m2_ref)
                  + branch(a3_ref, w3_ref, m3_ref)).astype(o_ref.dtype)


def _merge(a_na, a_df, a_cv, w_na, w_df, w_cv, p, seg, tm_pref, tn_pref):
    m = a_na.shape[0]
    d = w_na.shape[1]
    tm, tn = _tile(m, tm_pref), _tile(d, tn_pref)
    cb = {name: seg[name][0] // tn for name in ("merge_na", "merge_df", "merge_cv")}

    def a_spec(a):
        return pl.BlockSpec((tm, a.shape[1]), lambda i, j: (i, 0))

    def w_spec(w):
        return pl.BlockSpec((w.shape[0], tn), lambda i, j: (0, j))

    def m_spec(name):
        return pl.BlockSpec((tm, tn), lambda i, j: (i, cb[name] + j))

    return pl.pallas_call(
        _merge_kernel,
        grid=(m // tm, d // tn),
        in_specs=[a_spec(a_na), a_spec(a_df), a_spec(a_cv), w_spec(w_na), w_spec(w_df), w_spec(w_cv),
                  m_spec("merge_na"), m_spec("merge_df"), m_spec("merge_cv")],
        out_specs=pl.BlockSpec((tm, tn), lambda i, j: (i, j)),
        out_shape=jax.ShapeDtypeStruct((m, d), BF16),
        compiler_params=_params(("arbitrary", "arbitrary")),
        name="merge",
    )(a_na, a_df, a_cv, w_na, w_df, w_cv, p, p, p)


def _post_kernel(x_ref, z_ref, gate_ref, g_ref, b_ref, *rest, alpha, emit_h):
    if emit_h:
        sc_ref, sh_ref, o_ref, h_ref = rest
    else:
        (o_ref,) = rest
    v = alpha * x_ref[...] + gate_ref[...] * z_ref[...]
    y = _ln_rows(v, LN_EPS) * g_ref[...] + b_ref[...]
    o_ref[...] = y
    if emit_h:
        h_ref[...] = (_ln_rows(y, LN_EPS) * (1.0 + sc_ref[...]) + sh_ref[...]).astype(h_ref.dtype)


def _post(x, z, gate, g, b, alpha, next_mod):
    m, d = x.shape
    tm = _tile(m, 256)
    emit_h = next_mod is not None
    row = pl.BlockSpec((tm, d), lambda i: (i, 0))
    vec = pl.BlockSpec((1, d), lambda i: (0, 0))
    in_specs = [row, row, vec, vec, vec]
    args = [x, z, gate, g.reshape(1, d).astype(F32), b.reshape(1, d).astype(F32)]
    out_specs, out_shape = row, jax.ShapeDtypeStruct((m, d), F32)
    if emit_h:
        in_specs += [vec, vec]
        args += list(next_mod)
        out_specs = [row, row]
        out_shape = [out_shape, jax.ShapeDtypeStruct((m, d), BF16)]
    return pl.pallas_call(
        functools.partial(_post_kernel, alpha=alpha, emit_h=emit_h),
        grid=(m // tm,),
        in_specs=in_specs,
        out_specs=out_specs,
        out_shape=out_shape,
        compiler_params=_params(("arbitrary",)),
        name="post_ln",
    )(*args)


def _rope_tables(n_tokens):
    t = jnp.arange(n_tokens, dtype=jnp.int32)
    row = (t // GRID_W).astype(F32)
    col = (t % GRID_W).astype(F32)
    n_pairs_axis = HEAD_DIM // 4
    inv_freq = ROPE_BASE ** (-jnp.arange(n_pairs_axis, dtype=F32) / n_pairs_axis)
    ang = jnp.concatenate([row[:, None] * inv_freq, col[:, None] * inv_freq], -1)
    cos, sin = jnp.cos(ang), jnp.sin(ang)
    cosf = jnp.repeat(cos, 2, axis=-1)
    sins = jnp.stack([-sin, sin], -1).reshape(n_tokens, HEAD_DIM)
    return cosf, sins


def _segments(d, naw, dqw, dw, cw):
    sizes = (("na_q", naw), ("na_k", naw), ("na_v", naw), ("na_gate", naw),
             ("df_q", dqw), ("df_k", dqw), ("df_v", dw), ("df_gate", dw),
             ("cv_val", cw), ("cv_glu", cw), ("cv_gate", cw),
             ("merge_na", d), ("merge_df", d), ("merge_cv", d))
    seg, off = {}, 0
    for name, size in sizes:
        seg[name] = (off, off + size)
        off += size
    return seg, off


def kernel(x, c, ctx, c_ctx, w_ada, b_ada, w_in, b_in, na_rpb, diff_lq1, diff_lk1, diff_lq2, diff_lk2, diff_subln_g, conv_w, conv_b, conv_ln_g, conv_ln_b, w_proj_na, w_proj_diff, w_proj_conv, w_out, post_ln_g, post_ln_b):
    b, l, d = x.shape
    assert b == 1 and c.shape[0] == 1 and ctx.shape[0] == 1
    depth = w_ada.shape[0]
    naw, dw, cw = w_proj_na.shape[1], w_proj_diff.shape[1], w_proj_conv.shape[1]
    n_in = w_in.shape[-1]
    dqw = (n_in - 4 * naw - 2 * dw - 3 * cw - 3 * d) // 2
    seg, total = _segments(d, naw, dqw, dw, cw)
    assert total == n_in and dqw == dw
    rows = l // GRID_W
    kr = min(NA_KR, rows)
    alpha = (2.0 * depth) ** 0.25

    xl = x[0]
    xc = ctx[0]
    cc = jnp.zeros((8, d), F32).at[0].set(c[0]).at[1].set(c_ctx)
    mod = _ada(cc, w_ada, b_ada)

    def mod_vecs(i, r):
        m = mod[i, r]
        return m[None, :d], m[None, d:2 * d], m[None, 2 * d:]

    rope_tabs = _rope_tables(l)
    shift, scale, _ = mod_vecs(0, 0)
    h = _lnmod(xl, scale, shift)
    for i in range(depth):
        last = i == depth - 1
        lam_init = 0.8 - 0.6 * math.exp(-0.3 * i)
        _, _, gate = mod_vecs(i, 0)
        shift_c, scale_c, gate_c = mod_vecs(i, 1)
        w_in_i = w_in[i].astype(BF16)
        w_na, w_df, w_cv = w_proj_na[i].astype(BF16), w_proj_diff[i].astype(BF16), w_proj_conv[i].astype(BF16)
        w_o = w_out[i].astype(BF16)
        lam_params = (diff_lq1[i], diff_lk1[i], diff_lq2[i], diff_lk2[i])

        hc = _lnmod(xc, scale_c, shift_c)
        p = _inproj(h, w_in_i, b_in[i], seg, rope_tabs, 1024, 512)
        pc = _inproj(hc, w_in_i, b_in[i], seg, None, 256, 512)

        a_na = _na(p, pc, _na_bias_table(na_rpb[i], kr), seg, rows)
        a_df = _diff_attn(p, p, pc, lam_params, diff_subln_g[i], seg, lam_init, 512, 1024)
        a_cv = _conv(p, conv_w[i], conv_b[i], conv_ln_g[i], conv_ln_b[i], seg, 256)
        ym = _merge(a_na, a_df, a_cv, w_na, w_df, w_cv, p, seg, 512, 512)
        z = _matmul(ym, w_o, F32, 1024, 512)

        if not last:
            ac_na = _dense_attn(pc, seg)
            ac_df = _diff_attn(pc, pc, None, lam_params, diff_subln_g[i], seg, lam_init, 256, 256)
            ac_cv = _conv(pc, conv_w[i], conv_b[i], conv_ln_g[i], conv_ln_b[i], seg, 256)
            ymc = _merge(ac_na, ac_df, ac_cv, w_na, w_df, w_cv, pc, seg, 256, 512)
            zc = _matmul(ymc, w_o, F32, 256, 512)
            xc = _post(xc, zc, gate_c, post_ln_g[i], post_ln_b[i], alpha, None)
            shift_n, scale_n, _ = mod_vecs(i + 1, 0)
            xl, h = _post(xl, z, gate, post_ln_g[i], post_ln_b[i], alpha, (scale_n, shift_n))
        else:
            xl = _post(xl, z, gate, post_ln_g[i], post_ln_b[i], alpha, None)
    return xl[None]
```

```python
import functools
import math

import jax
import jax.numpy as jnp
from jax import lax
from jax.experimental import pallas as pl
from jax.experimental.pallas import tpu as pltpu

HEAD_DIM = 128
GRID_W = 64
NA_KR = 8
NA_KW = 16
CONV_K = 31
CONV_HALO = 16
ROPE_BASE = 10000.0
LN_EPS = 1e-6
DIFF_LN_EPS = 1e-5
CONV_LN_EPS = 1e-5
NEG_BIG = -1e30
LOG2E = math.log2(math.e)
QSCALE = HEAD_DIM ** -0.5 * LOG2E
DIFF_MAX_EXCESS = 64.0
V7X_VMEM_LIMIT = 56 * 1024 * 1024

F32 = jnp.float32
BF16 = jnp.bfloat16


def _tile(n, pref):
    t = min(n, pref)
    while n % t:
        t //= 2
    return t


def _params(sem):
    return pltpu.CompilerParams(dimension_semantics=sem, vmem_limit_bytes=V7X_VMEM_LIMIT)


def _silu(x):
    return x * jax.nn.sigmoid(x)


def _ln_rows(x, eps):
    mu = jnp.mean(x, -1, keepdims=True)
    xc = x - mu
    var = jnp.mean(xc * xc, -1, keepdims=True)
    return xc * lax.rsqrt(var + eps)


def _dot_nt(a, b):
    return lax.dot_general(a, b, (((1,), (1,)), ((), ())), preferred_element_type=F32)


def _ada_kernel(cc_ref, w_ref, b_ref, o_ref):
    s = _silu(cc_ref[...])
    o_ref[0] = jnp.dot(s.astype(BF16), w_ref[0].astype(BF16), preferred_element_type=F32) + b_ref[0]


def _ada(cc, w_ada, b_ada):
    depth, d, n = w_ada.shape
    tn = _tile(n, 512)
    return pl.pallas_call(
        _ada_kernel,
        grid=(depth, n // tn),
        in_specs=[pl.BlockSpec((8, d), lambda i, j: (0, 0)),
                  pl.BlockSpec((1, d, tn), lambda i, j: (i, 0, j)),
                  pl.BlockSpec((1, 1, tn), lambda i, j: (i, 0, j))],
        out_specs=pl.BlockSpec((1, 8, tn), lambda i, j: (i, 0, j)),
        out_shape=jax.ShapeDtypeStruct((depth, 8, n), F32),
        compiler_params=_params(("arbitrary", "arbitrary")),
        name="ada",
    )(cc, w_ada, b_ada.reshape(depth, 1, n))


def _lnmod_kernel(x_ref, sc_ref, sh_ref, o_ref):
    y = _ln_rows(x_ref[...], LN_EPS)
    o_ref[...] = (y * (1.0 + sc_ref[...]) + sh_ref[...]).astype(o_ref.dtype)


def _lnmod(x, scale, shift):
    m, d = x.shape
    tm = _tile(m, 256)
    return pl.pallas_call(
        _lnmod_kernel,
        grid=(m // tm,),
        in_specs=[pl.BlockSpec((tm, d), lambda i: (i, 0)),
                  pl.BlockSpec((1, d), lambda i: (0, 0)),
                  pl.BlockSpec((1, d), lambda i: (0, 0))],
        out_specs=pl.BlockSpec((tm, d), lambda i: (i, 0)),
        out_shape=jax.ShapeDtypeStruct((m, d), BF16),
        compiler_params=_params(("arbitrary",)),
        name="lnmod",
    )(x, scale, shift)


def _rope_tile(z, cosf, sins):
    lane = lax.broadcasted_iota(jnp.int32, cosf.shape, 1)
    even = (lane % 2) == 0
    outs = []
    for c0 in range(0, z.shape[1], HEAD_DIM):
        zc = z[:, c0:c0 + HEAD_DIM]
        sw = jnp.where(even, pltpu.roll(zc, HEAD_DIM - 1, 1), pltpu.roll(zc, 1, 1))
        outs.append(zc * cosf + sw * sins)
    return outs[0] if len(outs) == 1 else jnp.concatenate(outs, -1)


def _inproj_kernel(a_ref, w_ref, b_ref, *rest, tn, seg, qscale, rope):
    if rope:
        cos_ref, sin_ref, o_ref = rest
    else:
        (o_ref,) = rest
    acc = jnp.dot(a_ref[...], w_ref[...], preferred_element_type=F32) + b_ref[...]
    col = pl.program_id(1) * tn
    in_naq = (col >= seg["na_q"][0]) & (col < seg["na_q"][1])
    in_dfq = (col >= seg["df_q"][0]) & (col < seg["df_q"][1])
    in_dfk = (col >= seg["df_k"][0]) & (col < seg["df_k"][1])

    def rot(z):
        return _rope_tile(z, cos_ref[...], sin_ref[...]) if rope else z

    @pl.when(in_naq)
    def _():
        o_ref[...] = (acc * qscale).astype(o_ref.dtype)

    @pl.when(in_dfq)
    def _():
        o_ref[...] = (rot(acc) * qscale).astype(o_ref.dtype)

    @pl.when(in_dfk)
    def _():
        o_ref[...] = rot(acc).astype(o_ref.dtype)

    @pl.when(jnp.logical_not(in_naq | in_dfq | in_dfk))
    def _():
        o_ref[...] = acc.astype(o_ref.dtype)


def _inproj(a, w, b, seg, rope_tabs, tm_pref, tn_pref):
    m, k = a.shape
    n = w.shape[1]
    tm = _tile(m, tm_pref)
    tn = _tile(math.gcd(*[s[1] - s[0] for s in seg.values()]), tn_pref)
    rope = rope_tabs is not None
    in_specs = [pl.BlockSpec((tm, k), lambda i, j: (i, 0)),
                pl.BlockSpec((k, tn), lambda i, j: (0, j)),
                pl.BlockSpec((1, tn), lambda i, j: (0, j))]
    args = [a, w, b.reshape(1, n)]
    if rope:
        in_specs += [pl.BlockSpec((tm, HEAD_DIM), lambda i, j: (i, 0))] * 2
        args += list(rope_tabs)
    return pl.pallas_call(
        functools.partial(_inproj_kernel, tn=tn, seg=seg, qscale=QSCALE, rope=rope),
        grid=(m // tm, n // tn),
        in_specs=in_specs,
        out_specs=pl.BlockSpec((tm, tn), lambda i, j: (i, j)),
        out_shape=jax.ShapeDtypeStruct((m, n), BF16),
        compiler_params=_params(("arbitrary", "arbitrary")),
        name="inproj",
    )(*args)


def _mm_kernel(a_ref, w_ref, o_ref):
    o_ref[...] = jnp.dot(a_ref[...], w_ref[...], preferred_element_type=F32).astype(o_ref.dtype)


def _matmul(a, w, out_dtype, tm_pref, tn_pref):
    m, k = a.shape
    n = w.shape[1]
    tm, tn = _tile(m, tm_pref), _tile(n, tn_pref)
    return pl.pallas_call(
        _mm_kernel,
        grid=(m // tm, n // tn),
        in_specs=[pl.BlockSpec((tm, k), lambda i, j: (i, 0)),
                  pl.BlockSpec((k, tn), lambda i, j: (0, j))],
        out_specs=pl.BlockSpec((tm, tn), lambda i, j: (i, j)),
        out_shape=jax.ShapeDtypeStruct((m, n), out_dtype),
        compiler_params=_params(("arbitrary", "arbitrary")),
        name="outproj",
    )(a, w)


def _na_kernel(q_ref, *rest, kr, heads):
    k_refs, v_refs = rest[:kr], rest[kr:2 * kr]
    gate_ref, kc_ref, vc_ref, bias_ref, o_ref, ks_ref, vs_ref = rest[2 * kr:]
    w = q_ref.shape[1]
    for t in range(kr):
        ks_ref[t * w:(t + 1) * w, :] = k_refs[t][0]
        vs_ref[t * w:(t + 1) * w, :] = v_refs[t][0]
    for h in range(heads):
        sl = slice(h * HEAD_DIM, (h + 1) * HEAD_DIM)
        q = q_ref[0, :, sl]
        s = _dot_nt(q, ks_ref[:, sl]) + bias_ref[0, h]
        sc = _dot_nt(q, kc_ref[:, sl])
        m = jnp.maximum(jnp.max(s, -1, keepdims=True), jnp.max(sc, -1, keepdims=True))
        p = jnp.exp2(s - m)
        pc = jnp.exp2(sc - m)
        l = jnp.sum(p, -1, keepdims=True) + jnp.sum(pc, -1, keepdims=True)
        o = (jnp.dot(p.astype(BF16), vs_ref[:, sl], preferred_element_type=F32)
             + jnp.dot(pc.astype(BF16), vc_ref[:, sl], preferred_element_type=F32))
        g = gate_ref[0, :, sl].astype(F32)
        o_ref[0, :, sl] = (o / l * _silu(g)).astype(o_ref.dtype)


def _na_bias_table(rpb, kr):
    h = rpb.shape[0]
    cols = jnp.arange(GRID_W)
    col_start = jnp.clip(cols - NA_KW // 2, 0, GRID_W - NA_KW)
    kc = cols[None, :]
    valid = (kc >= col_start[:, None]) & (kc < col_start[:, None] + NA_KW)
    col_off = kc - cols[:, None] + (NA_KW - 1)
    onehot = ((col_off[None] == jnp.arange(2 * NA_KW - 1)[:, None, None]) & valid[None]).astype(F32)
    tz = jnp.einsum("hrc,cjk->hrjk", rpb.astype(F32) * LOG2E, onehot, precision=lax.Precision.HIGHEST)
    tz = jnp.where(valid[None, None], tz, NEG_BIG)
    tabs = [tz[:, NA_KR - 1 - d:NA_KR - 1 - d + kr] for d in range(kr)]
    b = jnp.stack(tabs)
    return jnp.transpose(b, (0, 1, 3, 2, 4)).reshape(kr, h, GRID_W, kr * GRID_W)


def _na(p, pc, bias, seg):
    l, n_in = p.shape
    c = pc.shape[0]
    naw = seg["na_q"][1] - seg["na_q"][0]
    heads = naw // HEAD_DIM
    rows = l // GRID_W
    kr = bias.shape[0]
    w = GRID_W
    p3 = p.reshape(rows, w, n_in)
    cb = {name: seg[name][0] // naw for name in ("na_q", "na_k", "na_v", "na_gate")}

    def r0(r):
        return jnp.clip(r - kr // 2, 0, rows - kr)

    def row_spec(name, t):
        return pl.BlockSpec((1, w, naw), lambda r: (r0(r) + t, 0, cb[name]))

    in_specs = ([pl.BlockSpec((1, w, naw), lambda r: (r, 0, cb["na_q"]))]
                + [row_spec("na_k", t) for t in range(kr)]
                + [row_spec("na_v", t) for t in range(kr)]
                + [pl.BlockSpec((1, w, naw), lambda r: (r, 0, cb["na_gate"])),
                   pl.BlockSpec((c, naw), lambda r: (0, cb["na_k"])),
                   pl.BlockSpec((c, naw), lambda r: (0, cb["na_v"])),
                   pl.BlockSpec((1, heads, w, kr * w), lambda r: (r - r0(r), 0, 0, 0))])
    out = pl.pallas_call(
        functools.partial(_na_kernel, kr=kr, heads=heads),
        grid=(rows,),
        in_specs=in_specs,
        out_specs=pl.BlockSpec((1, w, naw), lambda r: (r, 0, 0)),
        out_shape=jax.ShapeDtypeStruct((rows, w, naw), BF16),
        scratch_shapes=[pltpu.VMEM((kr * w, naw), BF16), pltpu.VMEM((kr * w, naw), BF16)],
        compiler_params=_params(("arbitrary",)),
        name="natten",
    )(p3, *([p3] * (2 * kr)), p3, pc, pc, bias)
    return out.reshape(l, naw)


def _dense_kernel(q_ref, k_ref, v_ref, gate_ref, o_ref):
    s = _dot_nt(q_ref[...], k_ref[...])
    p = jnp.exp2(s - jnp.max(s, -1, keepdims=True))
    l = jnp.sum(p, -1, keepdims=True)
    o = jnp.dot(p.astype(BF16), v_ref[...], preferred_element_type=F32)
    g = gate_ref[...].astype(F32)
    o_ref[...] = (o / l * _silu(g)).astype(o_ref.dtype)


def _dense_attn(pc, seg):
    c = pc.shape[0]
    naw = seg["na_q"][1] - seg["na_q"][0]
    heads = naw // HEAD_DIM
    cb = {name: seg[name][0] // HEAD_DIM for name in ("na_q", "na_k", "na_v", "na_gate")}

    def spec(name):
        return pl.BlockSpec((c, HEAD_DIM), lambda h: (0, cb[name] + h))

    return pl.pallas_call(
        _dense_kernel,
        grid=(heads,),
        in_specs=[spec("na_q"), spec("na_k"), spec("na_v"), spec("na_gate")],
        out_specs=pl.BlockSpec((c, HEAD_DIM), lambda h: (0, h)),
        out_shape=jax.ShapeDtypeStruct((c, naw), BF16),
        compiler_params=_params(("arbitrary",)),
        name="ctx_dense_attn",
    )(pc, pc, pc, pc)


def _diff_kernel(q_ref, k_ref, v_ref, *rest, nk, lam_init, has_ctx):
    if has_ctx:
        kc_ref, vc_ref = rest[:2]
        rest = rest[2:]
    gate_ref, lq1_ref, lk1_ref, lq2_ref, lk2_ref, g_ref, o_ref, m_ref, l_ref, acc_ref = rest
    ik = pl.program_id(2)
    tq = q_ref.shape[0]
    halves = [slice(i * HEAD_DIM, (i + 1) * HEAD_DIM) for i in range(2)]

    @pl.when(ik == 0)
    def _():
        m_ref[...] = jnp.full(m_ref.shape, NEG_BIG, F32)
        l_ref[...] = jnp.zeros(l_ref.shape, F32)
        acc_ref[...] = jnp.zeros(acc_ref.shape, F32)

    def pv_t(v_r, pts):
        pt = jnp.concatenate(pts, 1)
        return lax.dot_general(v_r[...], pt, (((0,), (0,)), ((), ())), preferred_element_type=F32)

    def scores_t(k_r):
        return [_dot_nt(k_r[:, sl], q_ref[:, sl]) for sl in halves]

    def update_exact(k_r, v_r):
        st = scores_t(k_r)
        alphas, pts = [], []
        for i in range(2):
            m_old = m_ref[i]
            m_new = jnp.maximum(m_old, jnp.max(st[i], 0, keepdims=True))
            alpha = jnp.exp2(m_old - m_new)
            p = jnp.exp2(st[i] - m_new)
            l_ref[i] = alpha * l_ref[i] + jnp.sum(p, 0, keepdims=True)
            m_ref[i] = m_new
            alphas.append(alpha)
            pts.append(p.astype(BF16))
        pv = pv_t(v_r, pts)
        for i in range(2):
            acc_ref[i] = alphas[i] * acc_ref[i] + pv[:, i * tq:(i + 1) * tq]

    def update(k_r, v_r):
        st = scores_t(k_r)
        sums, pts, excess = [], [], None
        for i in range(2):
            d = st[i] - m_ref[i]
            p = jnp.exp2(d)
            sums.append(jnp.sum(p, 0, keepdims=True))
            dmax = jnp.max(d, 0, keepdims=True)
            excess = dmax if excess is None else jnp.maximum(excess, dmax)
            pts.append(p.astype(BF16))
        pv = pv_t(v_r, pts)
        ok = jnp.max(excess) <= DIFF_MAX_EXCESS

        @pl.when(ok)
        def _():
            for i in range(2):
                l_ref[i] = l_ref[i] + sums[i]
                acc_ref[i] = acc_ref[i] + pv[:, i * tq:(i + 1) * tq]

        @pl.when(jnp.logical_not(ok))
        def _():
            update_exact(k_r, v_r)

    if has_ctx:
        @pl.when(ik == 0)
        def _():
            update_exact(kc_ref, vc_ref)

    update(k_ref, v_ref)

    @pl.when(ik == nk - 1)
    def _():
        lam = (jnp.exp(jnp.sum(lq1_ref[...] * lk1_ref[...], -1, keepdims=True))
               - jnp.exp(jnp.sum(lq2_ref[...] * lk2_ref[...], -1, keepdims=True)) + lam_init)
        ot = acc_ref[0] / l_ref[0] - lam * (acc_ref[1] / l_ref[1])
        ot = ot * lax.rsqrt(jnp.mean(ot * ot, 0, keepdims=True) + DIFF_LN_EPS)
        o = ot.T * g_ref[...]
        g = gate_ref[...].astype(F32)
        o_ref[...] = (o * (1.0 - lam_init) * _silu(g)).astype(o_ref.dtype)


def _diff_attn(pq, pk, pctx, lam_params, subln_g, seg, lam_init, tq_pref, tk_pref):
    lq, lk = pq.shape[0], pk.shape[0]
    hw = 2 * HEAD_DIM
    heads = (seg["df_v"][1] - seg["df_v"][0]) // hw
    cb = {name: seg[name][0] // hw for name in ("df_q", "df_k", "df_v", "df_gate")}
    tq, tk = _tile(lq, tq_pref), _tile(lk, tk_pref)
    nq, nk = lq // tq, lk // tk
    has_ctx = pctx is not None
    in_specs = [pl.BlockSpec((tq, hw), lambda h, i, j: (i, cb["df_q"] + h)),
                pl.BlockSpec((tk, hw), lambda h, i, j: (j, cb["df_k"] + h)),
                pl.BlockSpec((tk, hw), lambda h, i, j: (j, cb["df_v"] + h))]
    args = [pq, pk, pk]
    if has_ctx:
        c = pctx.shape[0]
        in_specs += [pl.BlockSpec((c, hw), lambda h, i, j: (0, cb["df_k"] + h)),
                     pl.BlockSpec((c, hw), lambda h, i, j: (0, cb["df_v"] + h))]
        args += [pctx, pctx]
    in_specs += [pl.BlockSpec((tq, hw), lambda h, i, j: (i, cb["df_gate"] + h))]
    in_specs += [pl.BlockSpec((1, HEAD_DIM), lambda h, i, j: (0, 0))] * 4
    in_specs += [pl.BlockSpec((1, hw), lambda h, i, j: (0, 0))]
    args += [pq] + [v.reshape(1, HEAD_DIM).astype(F32) for v in lam_params] + [subln_g.reshape(1, hw).astype(F32)]
    return pl.pallas_call(
        functools.partial(_diff_kernel, nk=nk, lam_init=lam_init, has_ctx=has_ctx),
        grid=(heads, nq, nk),
        in_specs=in_specs,
        out_specs=pl.BlockSpec((tq, hw), lambda h, i, j: (i, h)),
        out_shape=jax.ShapeDtypeStruct((lq, heads * hw), BF16),
        scratch_shapes=[pltpu.VMEM((2, 1, tq), F32), pltpu.VMEM((2, 1, tq), F32),
                        pltpu.VMEM((2, hw, tq), F32)],
        compiler_params=_params(("arbitrary", "arbitrary", "arbitrary")),
        name="diff_attn",
    )(*args)


def _conv_kernel(vp_ref, vc_ref, vn_ref, gp_ref, gc_ref, gn_ref, gate_ref, w_ref, b_ref, lg_ref, lb_ref,
                 o_ref, u_ref, y_ref, *, ts, nt, rb):
    i = pl.program_id(0)
    nch = u_ref.shape[0]

    def glu(v_r, g_r, c0):
        v = v_r[:, c0:c0 + HEAD_DIM].astype(F32)
        g = g_r[:, c0:c0 + HEAD_DIM].astype(F32)
        return v * jax.nn.sigmoid(g)

    has_prev = (i > 0).astype(F32)
    has_next = (i < nt - 1).astype(F32)
    for c in range(nch):
        c0 = c * HEAD_DIM
        u_ref[c, 0:CONV_HALO, :] = glu(vp_ref, gp_ref, c0) * has_prev
        u_ref[c, CONV_HALO:CONV_HALO + ts, :] = glu(vc_ref, gc_ref, c0)
        u_ref[c, CONV_HALO + ts:2 * CONV_HALO + ts, :] = glu(vn_ref, gn_ref, c0) * has_next

    base = CONV_HALO - CONV_K // 2

    def chunk(c, carry):
        for r0 in range(0, ts, rb):
            acc = jnp.zeros((rb, HEAD_DIM), F32)
            for j in range(CONV_K):
                acc = acc + w_ref[c, j:j + 1, :] * u_ref[c, pl.ds(base + r0 + j, rb), :]
            y_ref[c, r0:r0 + rb, :] = acc
        return carry

    lax.fori_loop(0, nch, chunk, 0)

    y = jnp.concatenate([y_ref[c] for c in range(nch)], -1) + b_ref[...]
    y = _ln_rows(y, CONV_LN_EPS) * lg_ref[...] + lb_ref[...]
    g = gate_ref[...].astype(F32)
    o_ref[...] = (_silu(y) * _silu(g)).astype(o_ref.dtype)


def _conv(p, conv_w, conv_b, ln_g, ln_b, seg, ts_pref):
    l = p.shape[0]
    cw = seg["cv_val"][1] - seg["cv_val"][0]
    nch = cw // HEAD_DIM
    ts = _tile(l, ts_pref)
    nt = l // ts
    hb = ts // CONV_HALO
    nhb = l // CONV_HALO
    cb = {name: seg[name][0] // cw for name in ("cv_val", "cv_glu", "cv_gate")}

    def specs(name):
        return [pl.BlockSpec((CONV_HALO, cw), lambda i: (jnp.maximum(i * hb - 1, 0), cb[name])),
                pl.BlockSpec((ts, cw), lambda i: (i, cb[name])),
                pl.BlockSpec((CONV_HALO, cw), lambda i: (jnp.minimum((i + 1) * hb, nhb - 1), cb[name]))]

    vec = pl.BlockSpec((1, cw), lambda i: (0, 0))
    w3 = jnp.transpose(conv_w.astype(F32).reshape(CONV_K, nch, HEAD_DIM), (1, 0, 2))
    return pl.pallas_call(
        functools.partial(_conv_kernel, ts=ts, nt=nt, rb=_tile(ts, 64)),
        grid=(nt,),
        in_specs=specs("cv_val") + specs("cv_glu") + [pl.BlockSpec((ts, cw), lambda i: (i, cb["cv_gate"])),
                                                      pl.BlockSpec((nch, CONV_K, HEAD_DIM), lambda i: (0, 0, 0)),
                                                      vec, vec, vec],
        out_specs=pl.BlockSpec((ts, cw), lambda i: (i, 0)),
        out_shape=jax.ShapeDtypeStruct((l, cw), BF16),
        scratch_shapes=[pltpu.VMEM((nch, ts + 2 * CONV_HALO, HEAD_DIM), F32),
                        pltpu.VMEM((nch, ts, HEAD_DIM), F32)],
        compiler_params=_params(("arbitrary",)),
        name="conformer_conv",
    )(p, p, p, p, p, p, p, w3, conv_b.reshape(1, cw).astype(F32), ln_g.reshape(1, cw).astype(F32),
      ln_b.reshape(1, cw).astype(F32))


def _merge_kernel(a1_ref, a2_ref, a3_ref, w1_ref, w2_ref, w3_ref, m1_ref, m2_ref, m3_ref, o_ref):
    def branch(a_ref, w_ref, m_ref):
        y = jnp.dot(a_ref[...], w_ref[...], preferred_element_type=F32)
        return jax.nn.sigmoid(m_ref[...].astype(F32)) * y

    o_ref[...] = (branch(a1_ref, w1_ref, m1_ref) + branch(a2_ref, w2_ref, m2_ref)
                  + branch(a3_ref, w3_ref, m3_ref)).astype(o_ref.dtype)


def _merge(a_na, a_df, a_cv, w_na, w_df, w_cv, p, seg, tm_pref, tn_pref):
    m = a_na.shape[0]
    d = w_na.shape[1]
    tm, tn = _tile(m, tm_pref), _tile(d, tn_pref)
    cb = {name: seg[name][0] // tn for name in ("merge_na", "merge_df", "merge_cv")}

    def a_spec(a):
        return pl.BlockSpec((tm, a.shape[1]), lambda i, j: (i, 0))

    def w_spec(w):
        return pl.BlockSpec((w.shape[0], tn), lambda i, j: (0, j))

    def m_spec(name):
        return pl.BlockSpec((tm, tn), lambda i, j: (i, cb[name] + j))

    return pl.pallas_call(
        _merge_kernel,
        grid=(m // tm, d // tn),
        in_specs=[a_spec(a_na), a_spec(a_df), a_spec(a_cv), w_spec(w_na), w_spec(w_df), w_spec(w_cv),
                  m_spec("merge_na"), m_spec("merge_df"), m_spec("merge_cv")],
        out_specs=pl.BlockSpec((tm, tn), lambda i, j: (i, j)),
        out_shape=jax.ShapeDtypeStruct((m, d), BF16),
        compiler_params=_params(("arbitrary", "arbitrary")),
        name="merge",
    )(a_na, a_df, a_cv, w_na, w_df, w_cv, p, p, p)


def _post_kernel(x_ref, z_ref, gate_ref, g_ref, b_ref, *rest, alpha, emit_h):
    if emit_h:
        sc_ref, sh_ref, o_ref, h_ref = rest
    else:
        (o_ref,) = rest
    v = alpha * x_ref[...] + gate_ref[...] * z_ref[...]
    y = _ln_rows(v, LN_EPS) * g_ref[...] + b_ref[...]
    o_ref[...] = y
    if emit_h:
        h_ref[...] = (_ln_rows(y, LN_EPS) * (1.0 + sc_ref[...]) + sh_ref[...]).astype(h_ref.dtype)


def _post(x, z, gate, g, b, alpha, next_mod):
    m, d = x.shape
    tm = _tile(m, 256)
    emit_h = next_mod is not None
    row = pl.BlockSpec((tm, d), lambda i: (i, 0))
    vec = pl.BlockSpec((1, d), lambda i: (0, 0))
    in_specs = [row, row, vec, vec, vec]
    args = [x, z, gate, g.reshape(1, d).astype(F32), b.reshape(1, d).astype(F32)]
    out_specs, out_shape = row, jax.ShapeDtypeStruct((m, d), F32)
    if emit_h:
        in_specs += [vec, vec]
        args += list(next_mod)
        out_specs = [row, row]
        out_shape = [out_shape, jax.ShapeDtypeStruct((m, d), BF16)]
    return pl.pallas_call(
        functools.partial(_post_kernel, alpha=alpha, emit_h=emit_h),
        grid=(m // tm,),
        in_specs=in_specs,
        out_specs=out_specs,
        out_shape=out_shape,
        compiler_params=_params(("arbitrary",)),
        name="post_ln",
    )(*args)


def _rope_tables(n_tokens):
    t = jnp.arange(n_tokens, dtype=jnp.int32)
    row = (t // GRID_W).astype(F32)
    col = (t % GRID_W).astype(F32)
    n_pairs_axis = HEAD_DIM // 4
    inv_freq = ROPE_BASE ** (-jnp.arange(n_pairs_axis, dtype=F32) / n_pairs_axis)
    ang = jnp.concatenate([row[:, None] * inv_freq, col[:, None] * inv_freq], -1)
    cos, sin = jnp.cos(ang), jnp.sin(ang)
    cosf = jnp.repeat(cos, 2, axis=-1)
    sins = jnp.stack([-sin, sin], -1).reshape(n_tokens, HEAD_DIM)
    return cosf, sins


def _segments(d, naw, dqw, dw, cw):
    sizes = (("na_q", naw), ("na_k", naw), ("na_v", naw), ("na_gate", naw),
             ("df_q", dqw), ("df_k", dqw), ("df_v", dw), ("df_gate", dw),
             ("cv_val", cw), ("cv_glu", cw), ("cv_gate", cw),
             ("merge_na", d), ("merge_df", d), ("merge_cv", d))
    seg, off = {}, 0
    for name, size in sizes:
        seg[name] = (off, off + size)
        off += size
    return seg, off


def kernel(x, c, ctx, c_ctx, w_ada, b_ada, w_in, b_in, na_rpb, diff_lq1, diff_lk1, diff_lq2, diff_lk2, diff_subln_g, conv_w, conv_b, conv_ln_g, conv_ln_b, w_proj_na, w_proj_diff, w_proj_conv, w_out, post_ln_g, post_ln_b):
    b, l, d = x.shape
    assert b == 1 and c.shape[0] == 1 and ctx.shape[0] == 1
    depth = w_ada.shape[0]
    naw, dw, cw = w_proj_na.shape[1], w_proj_diff.shape[1], w_proj_conv.shape[1]
    n_in = w_in.shape[-1]
    dqw = (n_in - 4 * naw - 2 * dw - 3 * cw - 3 * d) // 2
    seg, total = _segments(d, naw, dqw, dw, cw)
    assert total == n_in and dqw == dw
    rows = l // GRID_W
    kr = min(NA_KR, rows)
    alpha = (2.0 * depth) ** 0.25

    xl = x[0]
    xc = ctx[0]
    cc = jnp.zeros((8, d), F32).at[0].set(c[0]).at[1].set(c_ctx)
    mod = _ada(cc, w_ada, b_ada)

    def mod_vecs(i, r):
        m = mod[i, r]
        return m[None, :d], m[None, d:2 * d], m[None, 2 * d:]

    rope_tabs = _rope_tables(l)
    shift, scale, _ = mod_vecs(0, 0)
    h = _lnmod(xl, scale, shift)
    for i in range(depth):
        last = i == depth - 1
        lam_init = 0.8 - 0.6 * math.exp(-0.3 * i)
        _, _, gate = mod_vecs(i, 0)
        shift_c, scale_c, gate_c = mod_vecs(i, 1)
        w_in_i = w_in[i].astype(BF16)
        w_na, w_df, w_cv = w_proj_na[i].astype(BF16), w_proj_diff[i].astype(BF16), w_proj_conv[i].astype(BF16)
        w_o = w_out[i].astype(BF16)
        lam_params = (diff_lq1[i], diff_lk1[i], diff_lq2[i], diff_lk2[i])

        hc = _lnmod(xc, scale_c, shift_c)
        p = _inproj(h, w_in_i, b_in[i], seg, rope_tabs, 1024, 512)
        pc = _inproj(hc, w_in_i, b_in[i], seg, None, 256, 512)

        a_na = _na(p, pc, _na_bias_table(na_rpb[i], kr), seg)
        a_df = _diff_attn(p, p, pc, lam_params, diff_subln_g[i], seg, lam_init, 512, 2048)
        a_cv = _conv(p, conv_w[i], conv_b[i], conv_ln_g[i], conv_ln_b[i], seg, 256)
        ym = _merge(a_na, a_df, a_cv, w_na, w_df, w_cv, p, seg, 512, 512)
        z = _matmul(ym, w_o, F32, 1024, 512)

        if not last:
            ac_na = _dense_attn(pc, seg)
            ac_df = _diff_attn(pc, pc, None, lam_params, diff_subln_g[i], seg, lam_init, 256, 256)
            ac_cv = _conv(pc, conv_w[i], conv_b[i], conv_ln_g[i], conv_ln_b[i], seg, 256)
            ymc = _merge(ac_na, ac_df, ac_cv, w_na, w_df, w_cv, pc, seg, 256, 512)
            zc = _matmul(ymc, w_o, F32, 256, 512)
            xc = _post(xc, zc, gate_c, post_ln_g[i], post_ln_b[i], alpha, None)
            shift_n, scale_n, _ = mod_vecs(i + 1, 0)
            xl, h = _post(xl, z, gate, post_ln_g[i], post_ln_b[i], alpha, (scale_n, shift_n))
        else:
            xl = _post(xl, z, gate, post_ln_g[i], post_ln_b[i], alpha, None)
    return xl[None]
```

```python
import functools
import math

import jax
import jax.numpy as jnp
from jax import lax
from jax.experimental import pallas as pl
from jax.experimental.pallas import tpu as pltpu

HEAD_DIM = 128
GRID_W = 64
NA_KR = 8
NA_KW = 16
NA_BLOCK_ROWS = 8
NA_HEADS_PER_STEP = 4
CONV_K = 31
CONV_HALO = 16
ROPE_BASE = 10000.0
LN_EPS = 1e-6
DIFF_LN_EPS = 1e-5
CONV_LN_EPS = 1e-5
NEG_BIG = -1e30
LOG2E = math.log2(math.e)
QSCALE = HEAD_DIM ** -0.5 * LOG2E
DIFF_MAX_EXCESS = 64.0
V7X_VMEM_LIMIT = 56 * 1024 * 1024

F32 = jnp.float32
BF16 = jnp.bfloat16


def _tile(n, pref):
    t = min(n, pref)
    while n % t:
        t //= 2
    return t


def _params(sem):
    return pltpu.CompilerParams(dimension_semantics=sem, vmem_limit_bytes=V7X_VMEM_LIMIT)


def _silu(x):
    return x * jax.nn.sigmoid(x)


def _ln_rows(x, eps):
    mu = jnp.mean(x, -1, keepdims=True)
    xc = x - mu
    var = jnp.mean(xc * xc, -1, keepdims=True)
    return xc * lax.rsqrt(var + eps)


def _dot_nt(a, b):
    return lax.dot_general(a, b, (((1,), (1,)), ((), ())), preferred_element_type=F32)


def _ada_kernel(cc_ref, w_ref, b_ref, o_ref):
    s = _silu(cc_ref[...])
    o_ref[0] = jnp.dot(s.astype(BF16), w_ref[0].astype(BF16), preferred_element_type=F32) + b_ref[0]


def _ada(cc, w_ada, b_ada):
    depth, d, n = w_ada.shape
    tn = _tile(n, 512)
    return pl.pallas_call(
        _ada_kernel,
        grid=(depth, n // tn),
        in_specs=[pl.BlockSpec((8, d), lambda i, j: (0, 0)),
                  pl.BlockSpec((1, d, tn), lambda i, j: (i, 0, j)),
                  pl.BlockSpec((1, 1, tn), lambda i, j: (i, 0, j))],
        out_specs=pl.BlockSpec((1, 8, tn), lambda i, j: (i, 0, j)),
        out_shape=jax.ShapeDtypeStruct((depth, 8, n), F32),
        compiler_params=_params(("arbitrary", "arbitrary")),
        name="ada",
    )(cc, w_ada, b_ada.reshape(depth, 1, n))


def _lnmod_kernel(x_ref, sc_ref, sh_ref, o_ref):
    y = _ln_rows(x_ref[...], LN_EPS)
    o_ref[...] = (y * (1.0 + sc_ref[...]) + sh_ref[...]).astype(o_ref.dtype)


def _lnmod(x, scale, shift):
    m, d = x.shape
    tm = _tile(m, 256)
    return pl.pallas_call(
        _lnmod_kernel,
        grid=(m // tm,),
        in_specs=[pl.BlockSpec((tm, d), lambda i: (i, 0)),
                  pl.BlockSpec((1, d), lambda i: (0, 0)),
                  pl.BlockSpec((1, d), lambda i: (0, 0))],
        out_specs=pl.BlockSpec((tm, d), lambda i: (i, 0)),
        out_shape=jax.ShapeDtypeStruct((m, d), BF16),
        compiler_params=_params(("arbitrary",)),
        name="lnmod",
    )(x, scale, shift)


def _rope_tile(z, cosf, sins):
    lane = lax.broadcasted_iota(jnp.int32, cosf.shape, 1)
    even = (lane % 2) == 0
    outs = []
    for c0 in range(0, z.shape[1], HEAD_DIM):
        zc = z[:, c0:c0 + HEAD_DIM]
        sw = jnp.where(even, pltpu.roll(zc, HEAD_DIM - 1, 1), pltpu.roll(zc, 1, 1))
        outs.append(zc * cosf + sw * sins)
    return outs[0] if len(outs) == 1 else jnp.concatenate(outs, -1)


def _inproj_kernel(a_ref, w_ref, b_ref, *rest, tn, seg, qscale, rope):
    if rope:
        cos_ref, sin_ref, o_ref = rest
    else:
        (o_ref,) = rest
    acc = jnp.dot(a_ref[...], w_ref[...], preferred_element_type=F32) + b_ref[...]
    col = pl.program_id(1) * tn
    in_naq = (col >= seg["na_q"][0]) & (col < seg["na_q"][1])
    in_dfq = (col >= seg["df_q"][0]) & (col < seg["df_q"][1])
    in_dfk = (col >= seg["df_k"][0]) & (col < seg["df_k"][1])

    def rot(z):
        return _rope_tile(z, cos_ref[...], sin_ref[...]) if rope else z

    @pl.when(in_naq)
    def _():
        o_ref[...] = (acc * qscale).astype(o_ref.dtype)

    @pl.when(in_dfq)
    def _():
        o_ref[...] = (rot(acc) * qscale).astype(o_ref.dtype)

    @pl.when(in_dfk)
    def _():
        o_ref[...] = rot(acc).astype(o_ref.dtype)

    @pl.when(jnp.logical_not(in_naq | in_dfq | in_dfk))
    def _():
        o_ref[...] = acc.astype(o_ref.dtype)


def _inproj(a, w, b, seg, rope_tabs, tm_pref, tn_pref):
    m, k = a.shape
    n = w.shape[1]
    tm = _tile(m, tm_pref)
    tn = _tile(math.gcd(*[s[1] - s[0] for s in seg.values()]), tn_pref)
    rope = rope_tabs is not None
    in_specs = [pl.BlockSpec((tm, k), lambda i, j: (i, 0)),
                pl.BlockSpec((k, tn), lambda i, j: (0, j)),
                pl.BlockSpec((1, tn), lambda i, j: (0, j))]
    args = [a, w, b.reshape(1, n)]
    if rope:
        in_specs += [pl.BlockSpec((tm, HEAD_DIM), lambda i, j: (i, 0))] * 2
        args += list(rope_tabs)
    return pl.pallas_call(
        functools.partial(_inproj_kernel, tn=tn, seg=seg, qscale=QSCALE, rope=rope),
        grid=(m // tm, n // tn),
        in_specs=in_specs,
        out_specs=pl.BlockSpec((tm, tn), lambda i, j: (i, j)),
        out_shape=jax.ShapeDtypeStruct((m, n), BF16),
        compiler_params=_params(("arbitrary", "arbitrary")),
        name="inproj",
    )(*args)


def _mm_kernel(a_ref, w_ref, o_ref):
    o_ref[...] = jnp.dot(a_ref[...], w_ref[...], preferred_element_type=F32).astype(o_ref.dtype)


def _matmul(a, w, out_dtype, tm_pref, tn_pref):
    m, k = a.shape
    n = w.shape[1]
    tm, tn = _tile(m, tm_pref), _tile(n, tn_pref)
    return pl.pallas_call(
        _mm_kernel,
        grid=(m // tm, n // tn),
        in_specs=[pl.BlockSpec((tm, k), lambda i, j: (i, 0)),
                  pl.BlockSpec((k, tn), lambda i, j: (0, j))],
        out_specs=pl.BlockSpec((tm, tn), lambda i, j: (i, j)),
        out_shape=jax.ShapeDtypeStruct((m, n), out_dtype),
        compiler_params=_params(("arbitrary", "arbitrary")),
        name="outproj",
    )(a, w)


def _na_kernel(q_ref, *rest, nkb, nh):
    k_refs, v_refs = rest[:nkb], rest[nkb:2 * nkb]
    gate_ref, kc_ref, vc_ref, bias_ref, o_ref = rest[2 * nkb:]
    half = nkb // 2
    kh = k_refs[0].shape[0] * half
    nc = kh // HEAD_DIM
    heads = [slice(h * HEAD_DIM, (h + 1) * HEAD_DIM) for h in range(nh)]

    def halves(refs, sl):
        return [jnp.concatenate([r[:, sl] for r in refs[i * half:(i + 1) * half]], 0) for i in range(2)]

    scores = []
    for h, sl in enumerate(heads):
        q = q_ref[:, sl]
        ks = halves(k_refs, sl)
        scores.append([_dot_nt(q, ks[i]) + bias_ref[0, h, :, i * kh:(i + 1) * kh] for i in range(2)]
                      + [_dot_nt(q, kc_ref[:, sl])])
    probs, sums = [], []
    for s in scores:
        chunks = [x[:, c:c + HEAD_DIM] for x in s for c in range(0, x.shape[1], HEAD_DIM)]
        m = jnp.max(functools.reduce(jnp.maximum, chunks), -1, keepdims=True)
        ps = [jnp.exp2(ch - m) for ch in chunks]
        sums.append(jnp.sum(functools.reduce(jnp.add, ps), -1, keepdims=True))
        probs.append([jnp.concatenate([x.astype(BF16) for x in grp], -1)
                      for grp in (ps[:nc], ps[nc:2 * nc], ps[2 * nc:])])
    for h, sl in enumerate(heads):
        vs = halves(v_refs, sl)
        pb = probs[h]
        o = (jnp.dot(pb[0], vs[0], preferred_element_type=F32) + jnp.dot(pb[1], vs[1], preferred_element_type=F32)
             + jnp.dot(pb[2], vc_ref[:, sl], preferred_element_type=F32))
        g = gate_ref[:, sl].astype(F32)
        o_ref[:, sl] = (o / sums[h] * _silu(g)).astype(o_ref.dtype)


def _na_bias_table(rpb, rows, kr):
    h = rpb.shape[0]
    g, w = NA_BLOCK_ROWS, GRID_W
    ur = 2 * g
    cols = jnp.arange(w)
    col_start = jnp.clip(cols - NA_KW // 2, 0, w - NA_KW)
    kc = cols[None, :]
    valid = (kc >= col_start[:, None]) & (kc < col_start[:, None] + NA_KW)
    col_off = kc - cols[:, None] + (NA_KW - 1)
    onehot = ((col_off[None] == jnp.arange(2 * NA_KW - 1)[:, None, None]) & valid[None]).astype(F32)
    tz = jnp.einsum("hrc,cjk->hrjk", rpb.astype(F32) * LOG2E, onehot, precision=lax.Precision.HIGHEST)
    tz = jnp.where(valid[None, None], tz, NEG_BIG)
    nb = rows // g
    variants = []
    for b in (0, 1, nb - 1):
        r_blk = b * g
        u0 = min(max(r_blk - kr // 2, 0), rows - ur)
        per_row = []
        for a in range(g):
            r0 = min(max(r_blk + a - kr // 2, 0), rows - kr)
            t0 = r0 - u0
            d0 = r0 - (r_blk + a) + (NA_KR - 1)
            blk = jnp.transpose(tz[:, d0:d0 + kr], (0, 2, 1, 3)).reshape(h, w, kr * w)
            per_row.append(jnp.pad(blk, ((0, 0), (0, 0), (t0 * w, (ur - t0 - kr) * w)), constant_values=NEG_BIG))
        variants.append(jnp.stack(per_row, 1).reshape(h, g * w, ur * w))
    return jnp.stack(variants)


def _na(p, pc, bias, seg):
    l = p.shape[0]
    c = pc.shape[0]
    naw = seg["na_q"][1] - seg["na_q"][0]
    heads = naw // HEAD_DIM
    nh = NA_HEADS_PER_STEP if heads % NA_HEADS_PER_STEP == 0 else 1
    hw = nh * HEAD_DIM
    rows = l // GRID_W
    g = NA_BLOCK_ROWS
    nb = rows // g
    kr = min(NA_KR, rows)
    assert rows % g == 0 and rows >= 2 * g and kr % 2 == 0 and g % (kr // 2) == 0
    kb_rows = kr // 2
    kb = kb_rows * GRID_W
    nkb = 2 * g // kb_rows
    tq = g * GRID_W
    cb = {name: seg[name][0] // hw for name in ("na_q", "na_k", "na_v", "na_gate")}

    def u0_blk(b):
        return jnp.clip(b * g - kr // 2, 0, rows - 2 * g) // kb_rows

    def key_spec(name, i):
        return pl.BlockSpec((kb, hw), lambda h, b: (u0_blk(b) + i, cb[name] + h))

    def variant(b):
        return jnp.where(b == 0, 0, jnp.where(b == nb - 1, 2, 1))

    in_specs = ([pl.BlockSpec((tq, hw), lambda h, b: (b, cb["na_q"] + h))]
                + [key_spec("na_k", i) for i in range(nkb)]
                + [key_spec("na_v", i) for i in range(nkb)]
                + [pl.BlockSpec((tq, hw), lambda h, b: (b, cb["na_gate"] + h)),
                   pl.BlockSpec((c, hw), lambda h, b: (0, cb["na_k"] + h)),
                   pl.BlockSpec((c, hw), lambda h, b: (0, cb["na_v"] + h)),
                   pl.BlockSpec((1, nh, tq, 2 * tq), lambda h, b: (variant(b), h, 0, 0))])
    return pl.pallas_call(
        functools.partial(_na_kernel, nkb=nkb, nh=nh),
        grid=(heads // nh, nb),
        in_specs=in_specs,
        out_specs=pl.BlockSpec((tq, hw), lambda h, b: (b, h)),
        out_shape=jax.ShapeDtypeStruct((l, naw), BF16),
        compiler_params=_params(("arbitrary", "arbitrary")),
        name="natten",
    )(p, *([p] * (2 * nkb)), p, pc, pc, bias)


def _dense_kernel(q_ref, k_ref, v_ref, gate_ref, o_ref):
    s = _dot_nt(q_ref[...], k_ref[...])
    p = jnp.exp2(s - jnp.max(s, -1, keepdims=True))
    l = jnp.sum(p, -1, keepdims=True)
    o = jnp.dot(p.astype(BF16), v_ref[...], preferred_element_type=F32)
    g = gate_ref[...].astype(F32)
    o_ref[...] = (o / l * _silu(g)).astype(o_ref.dtype)


def _dense_attn(pc, seg):
    c = pc.shape[0]
    naw = seg["na_q"][1] - seg["na_q"][0]
    heads = naw // HEAD_DIM
    cb = {name: seg[name][0] // HEAD_DIM for name in ("na_q", "na_k", "na_v", "na_gate")}

    def spec(name):
        return pl.BlockSpec((c, HEAD_DIM), lambda h: (0, cb[name] + h))

    return pl.pallas_call(
        _dense_kernel,
        grid=(heads,),
        in_specs=[spec("na_q"), spec("na_k"), spec("na_v"), spec("na_gate")],
        out_specs=pl.BlockSpec((c, HEAD_DIM), lambda h: (0, h)),
        out_shape=jax.ShapeDtypeStruct((c, naw), BF16),
        compiler_params=_params(("arbitrary",)),
        name="ctx_dense_attn",
    )(pc, pc, pc, pc)


def _diff_kernel(q_ref, k_ref, v_ref, *rest, nk, lam_init, has_ctx):
    if has_ctx:
        kc_ref, vc_ref = rest[:2]
        rest = rest[2:]
    gate_ref, lq1_ref, lk1_ref, lq2_ref, lk2_ref, g_ref, o_ref, m_ref, l_ref, acc_ref = rest
    ik = pl.program_id(2)
    tq = q_ref.shape[0]
    halves = [slice(i * HEAD_DIM, (i + 1) * HEAD_DIM) for i in range(2)]

    @pl.when(ik == 0)
    def _():
        m_ref[...] = jnp.full(m_ref.shape, NEG_BIG, F32)
        l_ref[...] = jnp.zeros(l_ref.shape, F32)
        acc_ref[...] = jnp.zeros(acc_ref.shape, F32)

    def pv_t(v_r, pts):
        pt = jnp.concatenate(pts, 1)
        return lax.dot_general(v_r[...], pt, (((0,), (0,)), ((), ())), preferred_element_type=F32)

    def scores_t(k_r):
        return [_dot_nt(k_r[:, sl], q_ref[:, sl]) for sl in halves]

    def update_exact(k_r, v_r):
        st = scores_t(k_r)
        alphas, pts = [], []
        for i in range(2):
            m_old = m_ref[i]
            m_new = jnp.maximum(m_old, jnp.max(st[i], 0, keepdims=True))
            alpha = jnp.exp2(m_old - m_new)
            p = jnp.exp2(st[i] - m_new)
            l_ref[i] = alpha * l_ref[i] + jnp.sum(p, 0, keepdims=True)
            m_ref[i] = m_new
            alphas.append(alpha)
            pts.append(p.astype(BF16))
        pv = pv_t(v_r, pts)
        for i in range(2):
            acc_ref[i] = alphas[i] * acc_ref[i] + pv[:, i * tq:(i + 1) * tq]

    def update(k_r, v_r):
        st = scores_t(k_r)
        sums, pts, excess = [], [], None
        for i in range(2):
            d = st[i] - m_ref[i]
            p = jnp.exp2(d)
            sums.append(jnp.sum(p, 0, keepdims=True))
            dmax = jnp.max(d, 0, keepdims=True)
            excess = dmax if excess is None else jnp.maximum(excess, dmax)
            pts.append(p.astype(BF16))
        pv = pv_t(v_r, pts)
        ok = jnp.max(excess) <= DIFF_MAX_EXCESS

        @pl.when(ok)
        def _():
            for i in range(2):
                l_ref[i] = l_ref[i] + sums[i]
                acc_ref[i] = acc_ref[i] + pv[:, i * tq:(i + 1) * tq]

        @pl.when(jnp.logical_not(ok))
        def _():
            update_exact(k_r, v_r)

    if has_ctx:
        @pl.when(ik == 0)
        def _():
            update_exact(kc_ref, vc_ref)

    update(k_ref, v_ref)

    @pl.when(ik == nk - 1)
    def _():
        lam = (jnp.exp(jnp.sum(lq1_ref[...] * lk1_ref[...], -1, keepdims=True))
               - jnp.exp(jnp.sum(lq2_ref[...] * lk2_ref[...], -1, keepdims=True)) + lam_init)
        ot = acc_ref[0] / l_ref[0] - lam * (acc_ref[1] / l_ref[1])
        ot = ot * lax.rsqrt(jnp.mean(ot * ot, 0, keepdims=True) + DIFF_LN_EPS)
        o = ot.T * g_ref[...]
        g = gate_ref[...].astype(F32)
        o_ref[...] = (o * (1.0 - lam_init) * _silu(g)).astype(o_ref.dtype)


def _diff_attn(pq, pk, pctx, lam_params, subln_g, seg, lam_init, tq_pref, tk_pref):
    lq, lk = pq.shape[0], pk.shape[0]
    hw = 2 * HEAD_DIM
    heads = (seg["df_v"][1] - seg["df_v"][0]) // hw
    cb = {name: seg[name][0] // hw for name in ("df_q", "df_k", "df_v", "df_gate")}
    tq, tk = _tile(lq, tq_pref), _tile(lk, tk_pref)
    nq, nk = lq // tq, lk // tk
    has_ctx = pctx is not None
    in_specs = [pl.BlockSpec((tq, hw), lambda h, i, j: (i, cb["df_q"] + h)),
                pl.BlockSpec((tk, hw), lambda h, i, j: (j, cb["df_k"] + h)),
                pl.BlockSpec((tk, hw), lambda h, i, j: (j, cb["df_v"] + h))]
    args = [pq, pk, pk]
    if has_ctx:
        c = pctx.shape[0]
        in_specs += [pl.BlockSpec((c, hw), lambda h, i, j: (0, cb["df_k"] + h)),
                     pl.BlockSpec((c, hw), lambda h, i, j: (0, cb["df_v"] + h))]
        args += [pctx, pctx]
    in_specs += [pl.BlockSpec((tq, hw), lambda h, i, j: (i, cb["df_gate"] + h))]
    in_specs += [pl.BlockSpec((1, HEAD_DIM), lambda h, i, j: (0, 0))] * 4
    in_specs += [pl.BlockSpec((1, hw), lambda h, i, j: (0, 0))]
    args += [pq] + [v.reshape(1, HEAD_DIM).astype(F32) for v in lam_params] + [subln_g.reshape(1, hw).astype(F32)]
    return pl.pallas_call(
        functools.partial(_diff_kernel, nk=nk, lam_init=lam_init, has_ctx=has_ctx),
        grid=(heads, nq, nk),
        in_specs=in_specs,
        out_specs=pl.BlockSpec((tq, hw), lambda h, i, j: (i, h)),
        out_shape=jax.ShapeDtypeStruct((lq, heads * hw), BF16),
        scratch_shapes=[pltpu.VMEM((2, 1, tq), F32), pltpu.VMEM((2, 1, tq), F32),
                        pltpu.VMEM((2, hw, tq), F32)],
        compiler_params=_params(("arbitrary", "arbitrary", "arbitrary")),
        name="diff_attn",
    )(*args)


def _conv_kernel(vp_ref, vc_ref, vn_ref, gp_ref, gc_ref, gn_ref, gate_ref, w_ref, b_ref, lg_ref, lb_ref,
                 o_ref, u_ref, y_ref, *, ts, nt, rb):
    i = pl.program_id(0)
    nch = u_ref.shape[0]

    def glu(v_r, g_r, c0):
        v = v_r[:, c0:c0 + HEAD_DIM].astype(F32)
        g = g_r[:, c0:c0 + HEAD_DIM].astype(F32)
        return v * jax.nn.sigmoid(g)

    has_prev = (i > 0).astype(F32)
    has_next = (i < nt - 1).astype(F32)
    for c in range(nch):
        c0 = c * HEAD_DIM
        u_ref[c, 0:CONV_HALO, :] = glu(vp_ref, gp_ref, c0) * has_prev
        u_ref[c, CONV_HALO:CONV_HALO + ts, :] = glu(vc_ref, gc_ref, c0)
        u_ref[c, CONV_HALO + ts:2 * CONV_HALO + ts, :] = glu(vn_ref, gn_ref, c0) * has_next

    base = CONV_HALO - CONV_K // 2

    def chunk(c, carry):
        for r0 in range(0, ts, rb):
            acc = jnp.zeros((rb, HEAD_DIM), F32)
            for j in range(CONV_K):
                acc = acc + w_ref[c, j:j + 1, :] * u_ref[c, pl.ds(base + r0 + j, rb), :]
            y_ref[c, r0:r0 + rb, :] = acc
        return carry

    lax.fori_loop(0, nch, chunk, 0)

    y = jnp.concatenate([y_ref[c] for c in range(nch)], -1) + b_ref[...]
    y = _ln_rows(y, CONV_LN_EPS) * lg_ref[...] + lb_ref[...]
    g = gate_ref[...].astype(F32)
    o_ref[...] = (_silu(y) * _silu(g)).astype(o_ref.dtype)


def _conv(p, conv_w, conv_b, ln_g, ln_b, seg, ts_pref):
    l = p.shape[0]
    cw = seg["cv_val"][1] - seg["cv_val"][0]
    nch = cw // HEAD_DIM
    ts = _tile(l, ts_pref)
    nt = l // ts
    hb = ts // CONV_HALO
    nhb = l // CONV_HALO
    cb = {name: seg[name][0] // cw for name in ("cv_val", "cv_glu", "cv_gate")}

    def specs(name):
        return [pl.BlockSpec((CONV_HALO, cw), lambda i: (jnp.maximum(i * hb - 1, 0), cb[name])),
                pl.BlockSpec((ts, cw), lambda i: (i, cb[name])),
                pl.BlockSpec((CONV_HALO, cw), lambda i: (jnp.minimum((i + 1) * hb, nhb - 1), cb[name]))]

    vec = pl.BlockSpec((1, cw), lambda i: (0, 0))
    w3 = jnp.transpose(conv_w.astype(F32).reshape(CONV_K, nch, HEAD_DIM), (1, 0, 2))
    return pl.pallas_call(
        functools.partial(_conv_kernel, ts=ts, nt=nt, rb=_tile(ts, 64)),
        grid=(nt,),
        in_specs=specs("cv_val") + specs("cv_glu") + [pl.BlockSpec((ts, cw), lambda i: (i, cb["cv_gate"])),
                                                      pl.BlockSpec((nch, CONV_K, HEAD_DIM), lambda i: (0, 0, 0)),
                                                      vec, vec, vec],
        out_specs=pl.BlockSpec((ts, cw), lambda i: (i, 0)),
        out_shape=jax.ShapeDtypeStruct((l, cw), BF16),
        scratch_shapes=[pltpu.VMEM((nch, ts + 2 * CONV_HALO, HEAD_DIM), F32),
                        pltpu.VMEM((nch, ts, HEAD_DIM), F32)],
        compiler_params=_params(("arbitrary",)),
        name="conformer_conv",
    )(p, p, p, p, p, p, p, w3, conv_b.reshape(1, cw).astype(F32), ln_g.reshape(1, cw).astype(F32),
      ln_b.reshape(1, cw).astype(F32))


def _merge_kernel(a1_ref, a2_ref, a3_ref, w1_ref, w2_ref, w3_ref, m1_ref, m2_ref, m3_ref, o_ref):
    def branch(a_ref, w_ref, m_ref):
        y = jnp.dot(a_ref[...], w_ref[...], preferred_element_type=F32)
        return jax.nn.sigmoid(m_ref[...].astype(F32)) * y

    o_ref[...] = (branch(a1_ref, w1_ref, m1_ref) + branch(a2_ref, w2_ref, m2_ref)
                  + branch(a3_ref, w3_ref, m3_ref)).astype(o_ref.dtype)


def _merge(a_na, a_df, a_cv, w_na, w_df, w_cv, p, seg, tm_pref, tn_pref):
    m = a_na.shape[0]
    d = w_na.shape[1]
    tm, tn = _tile(m, tm_pref), _tile(d, tn_pref)
    cb = {name: seg[name][0] // tn for name in ("merge_na", "merge_df", "merge_cv")}

    def a_spec(a):
        return pl.BlockSpec((tm, a.shape[1]), lambda i, j: (i, 0))

    def w_spec(w):
        return pl.BlockSpec((w.shape[0], tn), lambda i, j: (0, j))

    def m_spec(name):
        return pl.BlockSpec((tm, tn), lambda i, j: (i, cb[name] + j))

    return pl.pallas_call(
        _merge_kernel,
        grid=(m // tm, d // tn),
        in_specs=[a_spec(a_na), a_spec(a_df), a_spec(a_cv), w_spec(w_na), w_spec(w_df), w_spec(w_cv),
                  m_spec("merge_na"), m_spec("merge_df"), m_spec("merge_cv")],
        out_specs=pl.BlockSpec((tm, tn), lambda i, j: (i, j)),
        out_shape=jax.ShapeDtypeStruct((m, d), BF16),
        compiler_params=_params(("arbitrary", "arbitrary")),
        name="merge",
    )(a_na, a_df, a_cv, w_na, w_df, w_cv, p, p, p)


def _post_kernel(x_ref, z_ref, gate_ref, g_ref, b_ref, *rest, alpha, emit_h):
    if emit_h:
        sc_ref, sh_ref, o_ref, h_ref = rest
    else:
        (o_ref,) = rest
    v = alpha * x_ref[...] + gate_ref[...] * z_ref[...]
    y = _ln_rows(v, LN_EPS) * g_ref[...] + b_ref[...]
    o_ref[...] = y
    if emit_h:
        h_ref[...] = (_ln_rows(y, LN_EPS) * (1.0 + sc_ref[...]) + sh_ref[...]).astype(h_ref.dtype)


def _post(x, z, gate, g, b, alpha, next_mod):
    m, d = x.shape
    tm = _tile(m, 256)
    emit_h = next_mod is not None
    row = pl.BlockSpec((tm, d), lambda i: (i, 0))
    vec = pl.BlockSpec((1, d), lambda i: (0, 0))
    in_specs = [row, row, vec, vec, vec]
    args = [x, z, gate, g.reshape(1, d).astype(F32), b.reshape(1, d).astype(F32)]
    out_specs, out_shape = row, jax.ShapeDtypeStruct((m, d), F32)
    if emit_h:
        in_specs += [vec, vec]
        args += list(next_mod)
        out_specs = [row, row]
        out_shape = [out_shape, jax.ShapeDtypeStruct((m, d), BF16)]
    return pl.pallas_call(
        functools.partial(_post_kernel, alpha=alpha, emit_h=emit_h),
        grid=(m // tm,),
        in_specs=in_specs,
        out_specs=out_specs,
        out_shape=out_shape,
        compiler_params=_params(("arbitrary",)),
        name="post_ln",
    )(*args)


def _rope_tables(n_tokens):
    t = jnp.arange(n_tokens, dtype=jnp.int32)
    row = (t // GRID_W).astype(F32)
    col = (t % GRID_W).astype(F32)
    n_pairs_axis = HEAD_DIM // 4
    inv_freq = ROPE_BASE ** (-jnp.arange(n_pairs_axis, dtype=F32) / n_pairs_axis)
    ang = jnp.concatenate([row[:, None] * inv_freq, col[:, None] * inv_freq], -1)
    cos, sin = jnp.cos(ang), jnp.sin(ang)
    cosf = jnp.repeat(cos, 2, axis=-1)
    sins = jnp.stack([-sin, sin], -1).reshape(n_tokens, HEAD_DIM)
    return cosf, sins


def _segments(d, naw, dqw, dw, cw):
    sizes = (("na_q", naw), ("na_k", naw), ("na_v", naw), ("na_gate", naw),
             ("df_q", dqw), ("df_k", dqw), ("df_v", dw), ("df_gate", dw),
             ("cv_val", cw), ("cv_glu", cw), ("cv_gate", cw),
             ("merge_na", d), ("merge_df", d), ("merge_cv", d))
    seg, off = {}, 0
    for name, size in sizes:
        seg[name] = (off, off + size)
        off += size
    return seg, off


def kernel(x, c, ctx, c_ctx, w_ada, b_ada, w_in, b_in, na_rpb, diff_lq1, diff_lk1, diff_lq2, diff_lk2, diff_subln_g, conv_w, conv_b, conv_ln_g, conv_ln_b, w_proj_na, w_proj_diff, w_proj_conv, w_out, post_ln_g, post_ln_b):
    b, l, d = x.shape
    assert b == 1 and c.shape[0] == 1 and ctx.shape[0] == 1
    depth = w_ada.shape[0]
    naw, dw, cw = w_proj_na.shape[1], w_proj_diff.shape[1], w_proj_conv.shape[1]
    n_in = w_in.shape[-1]
    dqw = (n_in - 4 * naw - 2 * dw - 3 * cw - 3 * d) // 2
    seg, total = _segments(d, naw, dqw, dw, cw)
    assert total == n_in and dqw == dw
    rows = l // GRID_W
    kr = min(NA_KR, rows)
    alpha = (2.0 * depth) ** 0.25

    xl = x[0]
    xc = ctx[0]
    cc = jnp.zeros((8, d), F32).at[0].set(c[0]).at[1].set(c_ctx)
    mod = _ada(cc, w_ada, b_ada)

    def mod_vecs(i, r):
        m = mod[i, r]
        return m[None, :d], m[None, d:2 * d], m[None, 2 * d:]

    rope_tabs = _rope_tables(l)
    shift, scale, _ = mod_vecs(0, 0)
    h = _lnmod(xl, scale, shift)
    for i in range(depth):
        last = i == depth - 1
        lam_init = 0.8 - 0.6 * math.exp(-0.3 * i)
        _, _, gate = mod_vecs(i, 0)
        shift_c, scale_c, gate_c = mod_vecs(i, 1)
        w_in_i = w_in[i].astype(BF16)
        w_na, w_df, w_cv = w_proj_na[i].astype(BF16), w_proj_diff[i].astype(BF16), w_proj_conv[i].astype(BF16)
        w_o = w_out[i].astype(BF16)
        lam_params = (diff_lq1[i], diff_lk1[i], diff_lq2[i], diff_lk2[i])

        hc = _lnmod(xc, scale_c, shift_c)
        p = _inproj(h, w_in_i, b_in[i], seg, rope_tabs, 1024, 512)
        pc = _inproj(hc, w_in_i, b_in[i], seg, None, 256, 512)

        a_na = _na(p, pc, _na_bias_table(na_rpb[i], rows, kr), seg)
        a_df = _diff_attn(p, p, pc, lam_params, diff_subln_g[i], seg, lam_init, 512, 4096)
        a_cv = _conv(p, conv_w[i], conv_b[i], conv_ln_g[i], conv_ln_b[i], seg, 256)
        ym = _merge(a_na, a_df, a_cv, w_na, w_df, w_cv, p, seg, 512, 512)
        z = _matmul(ym, w_o, F32, 1024, 512)

        if not last:
            ac_na = _dense_attn(pc, seg)
            ac_df = _diff_attn(pc, pc, None, lam_params, diff_subln_g[i], seg, lam_init, 256, 256)
            ac_cv = _conv(pc, conv_w[i], conv_b[i], conv_ln_g[i], conv_ln_b[i], seg, 256)
            ymc = _merge(ac_na, ac_df, ac_cv, w_na, w_df, w_cv, pc, seg, 256, 512)
            zc = _matmul(ymc, w_o, F32, 256, 512)
            xc = _post(xc, zc, gate_c, post_ln_g[i], post_ln_b[i], alpha, None)
            shift_n, scale_n, _ = mod_vecs(i + 1, 0)
            xl, h = _post(xl, z, gate, post_ln_g[i], post_ln_b[i], alpha, (scale_n, shift_n))
        else:
            xl = _post(xl, z, gate, post_ln_g[i], post_ln_b[i], alpha, None)
    return xl[None]
```

```python
import functools
import math

import jax
import jax.numpy as jnp
from jax import lax
from jax.experimental import pallas as pl
from jax.experimental.pallas import tpu as pltpu

HEAD_DIM = 128
GRID_W = 64
NA_KR = 8
NA_KW = 16
NA_BLOCK_ROWS = 8
NA_HEADS_PER_STEP = 4
CONV_K = 31
CONV_HALO = 16
ROPE_BASE = 10000.0
LN_EPS = 1e-6
DIFF_LN_EPS = 1e-5
CONV_LN_EPS = 1e-5
NEG_BIG = -1e30
LOG2E = math.log2(math.e)
QSCALE = HEAD_DIM ** -0.5 * LOG2E
DIFF_MAX_EXCESS = 64.0
V7X_VMEM_LIMIT = 56 * 1024 * 1024

F32 = jnp.float32
BF16 = jnp.bfloat16


def _tile(n, pref):
    t = min(n, pref)
    while n % t:
        t //= 2
    return t


def _params(sem):
    return pltpu.CompilerParams(dimension_semantics=sem, vmem_limit_bytes=V7X_VMEM_LIMIT)


def _silu(x):
    return x * jax.nn.sigmoid(x)


def _ln_rows(x, eps):
    mu = jnp.mean(x, -1, keepdims=True)
    xc = x - mu
    var = jnp.mean(xc * xc, -1, keepdims=True)
    return xc * lax.rsqrt(var + eps)


def _dot_nt(a, b):
    return lax.dot_general(a, b, (((1,), (1,)), ((), ())), preferred_element_type=F32)


def _ada_kernel(cc_ref, w_ref, b_ref, o_ref):
    s = _silu(cc_ref[...])
    o_ref[0] = jnp.dot(s.astype(BF16), w_ref[0].astype(BF16), preferred_element_type=F32) + b_ref[0]


def _ada(cc, w_ada, b_ada):
    depth, d, n = w_ada.shape
    tn = _tile(n, 512)
    return pl.pallas_call(
        _ada_kernel,
        grid=(depth, n // tn),
        in_specs=[pl.BlockSpec((8, d), lambda i, j: (0, 0)),
                  pl.BlockSpec((1, d, tn), lambda i, j: (i, 0, j)),
                  pl.BlockSpec((1, 1, tn), lambda i, j: (i, 0, j))],
        out_specs=pl.BlockSpec((1, 8, tn), lambda i, j: (i, 0, j)),
        out_shape=jax.ShapeDtypeStruct((depth, 8, n), F32),
        compiler_params=_params(("arbitrary", "arbitrary")),
        name="ada",
    )(cc, w_ada, b_ada.reshape(depth, 1, n))


def _lnmod_kernel(x_ref, sc_ref, sh_ref, o_ref):
    y = _ln_rows(x_ref[...], LN_EPS)
    o_ref[...] = (y * (1.0 + sc_ref[...]) + sh_ref[...]).astype(o_ref.dtype)


def _lnmod(x, scale, shift):
    m, d = x.shape
    tm = _tile(m, 256)
    return pl.pallas_call(
        _lnmod_kernel,
        grid=(m // tm,),
        in_specs=[pl.BlockSpec((tm, d), lambda i: (i, 0)),
                  pl.BlockSpec((1, d), lambda i: (0, 0)),
                  pl.BlockSpec((1, d), lambda i: (0, 0))],
        out_specs=pl.BlockSpec((tm, d), lambda i: (i, 0)),
        out_shape=jax.ShapeDtypeStruct((m, d), BF16),
        compiler_params=_params(("arbitrary",)),
        name="lnmod",
    )(x, scale, shift)


def _rope_tile(z, cosf, sins):
    lane = lax.broadcasted_iota(jnp.int32, cosf.shape, 1)
    even = (lane % 2) == 0
    outs = []
    for c0 in range(0, z.shape[1], HEAD_DIM):
        zc = z[:, c0:c0 + HEAD_DIM]
        sw = jnp.where(even, pltpu.roll(zc, HEAD_DIM - 1, 1), pltpu.roll(zc, 1, 1))
        outs.append(zc * cosf + sw * sins)
    return outs[0] if len(outs) == 1 else jnp.concatenate(outs, -1)


def _inproj_kernel(a_ref, w_ref, b_ref, *rest, tn, seg, qscale, rope):
    if rope:
        cos_ref, sin_ref, o_ref = rest
    else:
        (o_ref,) = rest
    acc = jnp.dot(a_ref[...], w_ref[...].astype(BF16), preferred_element_type=F32) + b_ref[...]
    col = pl.program_id(1) * tn
    in_naq = (col >= seg["na_q"][0]) & (col < seg["na_q"][1])
    in_dfq = (col >= seg["df_q"][0]) & (col < seg["df_q"][1])
    in_dfk = (col >= seg["df_k"][0]) & (col < seg["df_k"][1])

    def rot(z):
        return _rope_tile(z, cos_ref[...], sin_ref[...]) if rope else z

    @pl.when(in_naq)
    def _():
        o_ref[...] = (acc * qscale).astype(o_ref.dtype)

    @pl.when(in_dfq)
    def _():
        o_ref[...] = (rot(acc) * qscale).astype(o_ref.dtype)

    @pl.when(in_dfk)
    def _():
        o_ref[...] = rot(acc).astype(o_ref.dtype)

    @pl.when(jnp.logical_not(in_naq | in_dfq | in_dfk))
    def _():
        o_ref[...] = acc.astype(o_ref.dtype)


def _inproj(a, w_all, layer, b, seg, rope_tabs, tm_pref, tn_pref):
    m, k = a.shape
    n = w_all.shape[2]
    tm = _tile(m, tm_pref)
    tn = _tile(math.gcd(*[s[1] - s[0] for s in seg.values()]), tn_pref)
    rope = rope_tabs is not None
    in_specs = [pl.BlockSpec((tm, k), lambda i, j: (i, 0)),
                pl.BlockSpec((None, k, tn), lambda i, j: (layer, 0, j)),
                pl.BlockSpec((1, tn), lambda i, j: (0, j))]
    args = [a, w_all, b.reshape(1, n)]
    if rope:
        in_specs += [pl.BlockSpec((tm, HEAD_DIM), lambda i, j: (i, 0))] * 2
        args += list(rope_tabs)
    return pl.pallas_call(
        functools.partial(_inproj_kernel, tn=tn, seg=seg, qscale=QSCALE, rope=rope),
        grid=(m // tm, n // tn),
        in_specs=in_specs,
        out_specs=pl.BlockSpec((tm, tn), lambda i, j: (i, j)),
        out_shape=jax.ShapeDtypeStruct((m, n), BF16),
        compiler_params=_params(("arbitrary", "arbitrary")),
        name="inproj",
    )(*args)


def _mm_kernel(a_ref, w_ref, o_ref):
    o_ref[...] = jnp.dot(a_ref[...], w_ref[...].astype(BF16), preferred_element_type=F32).astype(o_ref.dtype)


def _matmul(a, w_all, layer, out_dtype, tm_pref, tn_pref):
    m, k = a.shape
    n = w_all.shape[2]
    tm, tn = _tile(m, tm_pref), _tile(n, tn_pref)
    return pl.pallas_call(
        _mm_kernel,
        grid=(m // tm, n // tn),
        in_specs=[pl.BlockSpec((tm, k), lambda i, j: (i, 0)),
                  pl.BlockSpec((None, k, tn), lambda i, j: (layer, 0, j))],
        out_specs=pl.BlockSpec((tm, tn), lambda i, j: (i, j)),
        out_shape=jax.ShapeDtypeStruct((m, n), out_dtype),
        compiler_params=_params(("arbitrary", "arbitrary")),
        name="outproj",
    )(a, w_all)


def _na_kernel(q_ref, *rest, nkb, nh):
    k_refs, v_refs = rest[:nkb], rest[nkb:2 * nkb]
    gate_ref, kc_ref, vc_ref, bias_ref, o_ref = rest[2 * nkb:]
    half = nkb // 2
    kh = k_refs[0].shape[0] * half
    nc = kh // HEAD_DIM
    heads = [slice(h * HEAD_DIM, (h + 1) * HEAD_DIM) for h in range(nh)]

    def halves(refs, sl):
        return [jnp.concatenate([r[:, sl] for r in refs[i * half:(i + 1) * half]], 0) for i in range(2)]

    scores = []
    for h, sl in enumerate(heads):
        q = q_ref[:, sl]
        ks = halves(k_refs, sl)
        scores.append([_dot_nt(q, ks[i]) + bias_ref[0, h, :, i * kh:(i + 1) * kh] for i in range(2)]
                      + [_dot_nt(q, kc_ref[:, sl])])
    probs, sums = [], []
    for s in scores:
        chunks = [x[:, c:c + HEAD_DIM] for x in s for c in range(0, x.shape[1], HEAD_DIM)]
        m = jnp.max(functools.reduce(jnp.maximum, chunks), -1, keepdims=True)
        ps = [jnp.exp2(ch - m) for ch in chunks]
        sums.append(jnp.sum(functools.reduce(jnp.add, ps), -1, keepdims=True))
        probs.append([jnp.concatenate([x.astype(BF16) for x in grp], -1)
                      for grp in (ps[:nc], ps[nc:2 * nc], ps[2 * nc:])])
    for h, sl in enumerate(heads):
        vs = halves(v_refs, sl)
        pb = probs[h]
        o = (jnp.dot(pb[0], vs[0], preferred_element_type=F32) + jnp.dot(pb[1], vs[1], preferred_element_type=F32)
             + jnp.dot(pb[2], vc_ref[:, sl], preferred_element_type=F32))
        g = gate_ref[:, sl].astype(F32)
        o_ref[:, sl] = (o / sums[h] * _silu(g)).astype(o_ref.dtype)


def _na_bias_table(rpb, rows, kr):
    h = rpb.shape[0]
    g, w = NA_BLOCK_ROWS, GRID_W
    ur = 2 * g
    cols = jnp.arange(w)
    col_start = jnp.clip(cols - NA_KW // 2, 0, w - NA_KW)
    kc = cols[None, :]
    valid = (kc >= col_start[:, None]) & (kc < col_start[:, None] + NA_KW)
    col_off = kc - cols[:, None] + (NA_KW - 1)
    onehot = ((col_off[None] == jnp.arange(2 * NA_KW - 1)[:, None, None]) & valid[None]).astype(F32)
    tz = jnp.einsum("hrc,cjk->hrjk", rpb.astype(F32) * LOG2E, onehot, precision=lax.Precision.HIGHEST)
    tz = jnp.where(valid[None, None], tz, NEG_BIG)
    nb = rows // g
    variants = []
    for b in (0, 1, nb - 1):
        r_blk = b * g
        u0 = min(max(r_blk - kr // 2, 0), rows - ur)
        per_row = []
        for a in range(g):
            r0 = min(max(r_blk + a - kr // 2, 0), rows - kr)
            t0 = r0 - u0
            d0 = r0 - (r_blk + a) + (NA_KR - 1)
            blk = jnp.transpose(tz[:, d0:d0 + kr], (0, 2, 1, 3)).reshape(h, w, kr * w)
            per_row.append(jnp.pad(blk, ((0, 0), (0, 0), (t0 * w, (ur - t0 - kr) * w)), constant_values=NEG_BIG))
        variants.append(jnp.stack(per_row, 1).reshape(h, g * w, ur * w))
    return jnp.stack(variants)


def _na(p, pc, bias, seg):
    l = p.shape[0]
    c = pc.shape[0]
    naw = seg["na_q"][1] - seg["na_q"][0]
    heads = naw // HEAD_DIM
    nh = NA_HEADS_PER_STEP if heads % NA_HEADS_PER_STEP == 0 else 1
    hw = nh * HEAD_DIM
    rows = l // GRID_W
    g = NA_BLOCK_ROWS
    nb = rows // g
    kr = min(NA_KR, rows)
    assert rows % g == 0 and rows >= 2 * g and kr % 2 == 0 and g % (kr // 2) == 0
    kb_rows = kr // 2
    kb = kb_rows * GRID_W
    nkb = 2 * g // kb_rows
    tq = g * GRID_W
    cb = {name: seg[name][0] // hw for name in ("na_q", "na_k", "na_v", "na_gate")}

    def u0_blk(b):
        return jnp.clip(b * g - kr // 2, 0, rows - 2 * g) // kb_rows

    def key_spec(name, i):
        return pl.BlockSpec((kb, hw), lambda h, b: (u0_blk(b) + i, cb[name] + h))

    def variant(b):
        return jnp.where(b == 0, 0, jnp.where(b == nb - 1, 2, 1))

    in_specs = ([pl.BlockSpec((tq, hw), lambda h, b: (b, cb["na_q"] + h))]
                + [key_spec("na_k", i) for i in range(nkb)]
                + [key_spec("na_v", i) for i in range(nkb)]
                + [pl.BlockSpec((tq, hw), lambda h, b: (b, cb["na_gate"] + h)),
                   pl.BlockSpec((c, hw), lambda h, b: (0, cb["na_k"] + h)),
                   pl.BlockSpec((c, hw), lambda h, b: (0, cb["na_v"] + h)),
                   pl.BlockSpec((1, nh, tq, 2 * tq), lambda h, b: (variant(b), h, 0, 0))])
    return pl.pallas_call(
        functools.partial(_na_kernel, nkb=nkb, nh=nh),
        grid=(heads // nh, nb),
        in_specs=in_specs,
        out_specs=pl.BlockSpec((tq, hw), lambda h, b: (b, h)),
        out_shape=jax.ShapeDtypeStruct((l, naw), BF16),
        compiler_params=_params(("arbitrary", "arbitrary")),
        name="natten",
    )(p, *([p] * (2 * nkb)), p, pc, pc, bias)


def _dense_kernel(q_ref, k_ref, v_ref, gate_ref, o_ref):
    s = _dot_nt(q_ref[...], k_ref[...])
    p = jnp.exp2(s - jnp.max(s, -1, keepdims=True))
    l = jnp.sum(p, -1, keepdims=True)
    o = jnp.dot(p.astype(BF16), v_ref[...], preferred_element_type=F32)
    g = gate_ref[...].astype(F32)
    o_ref[...] = (o / l * _silu(g)).astype(o_ref.dtype)


def _dense_attn(pc, seg):
    c = pc.shape[0]
    naw = seg["na_q"][1] - seg["na_q"][0]
    heads = naw // HEAD_DIM
    cb = {name: seg[name][0] // HEAD_DIM for name in ("na_q", "na_k", "na_v", "na_gate")}

    def spec(name):
        return pl.BlockSpec((c, HEAD_DIM), lambda h: (0, cb[name] + h))

    return pl.pallas_call(
        _dense_kernel,
        grid=(heads,),
        in_specs=[spec("na_q"), spec("na_k"), spec("na_v"), spec("na_gate")],
        out_specs=pl.BlockSpec((c, HEAD_DIM), lambda h: (0, h)),
        out_shape=jax.ShapeDtypeStruct((c, naw), BF16),
        compiler_params=_params(("arbitrary",)),
        name="ctx_dense_attn",
    )(pc, pc, pc, pc)


def _diff_kernel(q_ref, k_ref, v_ref, *rest, nk, lam_init, has_ctx):
    if has_ctx:
        kc_ref, vc_ref = rest[:2]
        rest = rest[2:]
    gate_ref, lq1_ref, lk1_ref, lq2_ref, lk2_ref, g_ref, o_ref, m_ref, l_ref, acc_ref = rest
    ik = pl.program_id(2)
    tq = q_ref.shape[0]
    halves = [slice(i * HEAD_DIM, (i + 1) * HEAD_DIM) for i in range(2)]

    @pl.when(ik == 0)
    def _():
        m_ref[...] = jnp.full(m_ref.shape, NEG_BIG, F32)
        l_ref[...] = jnp.zeros(l_ref.shape, F32)
        acc_ref[...] = jnp.zeros(acc_ref.shape, F32)

    def pv_t(v_r, pts):
        pt = jnp.concatenate(pts, 1)
        return lax.dot_general(v_r[...], pt, (((0,), (0,)), ((), ())), preferred_element_type=F32)

    def scores_t(k_r):
        return [_dot_nt(k_r[:, sl], q_ref[:, sl]) for sl in halves]

    def update_exact(k_r, v_r):
        st = scores_t(k_r)
        alphas, pts = [], []
        for i in range(2):
            m_old = m_ref[i]
            m_new = jnp.maximum(m_old, jnp.max(st[i], 0, keepdims=True))
            alpha = jnp.exp2(m_old - m_new)
            p = jnp.exp2(st[i] - m_new)
            l_ref[i] = alpha * l_ref[i] + jnp.sum(p, 0, keepdims=True)
            m_ref[i] = m_new
            alphas.append(alpha)
            pts.append(p.astype(BF16))
        pv = pv_t(v_r, pts)
        for i in range(2):
            acc_ref[i] = alphas[i] * acc_ref[i] + pv[:, i * tq:(i + 1) * tq]

    def update(k_r, v_r):
        st = scores_t(k_r)
        sums, pts, excess = [], [], None
        for i in range(2):
            d = st[i] - m_ref[i]
            p = jnp.exp2(d)
            sums.append(jnp.sum(p, 0, keepdims=True))
            dmax = jnp.max(d, 0, keepdims=True)
            excess = dmax if excess is None else jnp.maximum(excess, dmax)
            pts.append(p.astype(BF16))
        pv = pv_t(v_r, pts)
        ok = jnp.max(excess) <= DIFF_MAX_EXCESS

        @pl.when(ok)
        def _():
            for i in range(2):
                l_ref[i] = l_ref[i] + sums[i]
                acc_ref[i] = acc_ref[i] + pv[:, i * tq:(i + 1) * tq]

        @pl.when(jnp.logical_not(ok))
        def _():
            update_exact(k_r, v_r)

    if has_ctx:
        @pl.when(ik == 0)
        def _():
            update_exact(kc_ref, vc_ref)

    update(k_ref, v_ref)

    @pl.when(ik == nk - 1)
    def _():
        lam = (jnp.exp(jnp.sum(lq1_ref[...] * lk1_ref[...], -1, keepdims=True))
               - jnp.exp(jnp.sum(lq2_ref[...] * lk2_ref[...], -1, keepdims=True)) + lam_init)
        ot = acc_ref[0] / l_ref[0] - lam * (acc_ref[1] / l_ref[1])
        ot = ot * lax.rsqrt(jnp.mean(ot * ot, 0, keepdims=True) + DIFF_LN_EPS)
        o = ot.T * g_ref[...]
        g = gate_ref[...].astype(F32)
        o_ref[...] = (o * (1.0 - lam_init) * _silu(g)).astype(o_ref.dtype)


def _diff_attn(pq, pk, pctx, lam_params, subln_g, seg, lam_init, tq_pref, tk_pref):
    lq, lk = pq.shape[0], pk.shape[0]
    hw = 2 * HEAD_DIM
    heads = (seg["df_v"][1] - seg["df_v"][0]) // hw
    cb = {name: seg[name][0] // hw for name in ("df_q", "df_k", "df_v", "df_gate")}
    tq, tk = _tile(lq, tq_pref), _tile(lk, tk_pref)
    nq, nk = lq // tq, lk // tk
    has_ctx = pctx is not None
    in_specs = [pl.BlockSpec((tq, hw), lambda h, i, j: (i, cb["df_q"] + h)),
                pl.BlockSpec((tk, hw), lambda h, i, j: (j, cb["df_k"] + h)),
                pl.BlockSpec((tk, hw), lambda h, i, j: (j, cb["df_v"] + h))]
    args = [pq, pk, pk]
    if has_ctx:
        c = pctx.shape[0]
        in_specs += [pl.BlockSpec((c, hw), lambda h, i, j: (0, cb["df_k"] + h)),
                     pl.BlockSpec((c, hw), lambda h, i, j: (0, cb["df_v"] + h))]
        args += [pctx, pctx]
    in_specs += [pl.BlockSpec((tq, hw), lambda h, i, j: (i, cb["df_gate"] + h))]
    in_specs += [pl.BlockSpec((1, HEAD_DIM), lambda h, i, j: (0, 0))] * 4
    in_specs += [pl.BlockSpec((1, hw), lambda h, i, j: (0, 0))]
    args += [pq] + [v.reshape(1, HEAD_DIM).astype(F32) for v in lam_params] + [subln_g.reshape(1, hw).astype(F32)]
    return pl.pallas_call(
        functools.partial(_diff_kernel, nk=nk, lam_init=lam_init, has_ctx=has_ctx),
        grid=(heads, nq, nk),
        in_specs=in_specs,
        out_specs=pl.BlockSpec((tq, hw), lambda h, i, j: (i, h)),
        out_shape=jax.ShapeDtypeStruct((lq, heads * hw), BF16),
        scratch_shapes=[pltpu.VMEM((2, 1, tq), F32), pltpu.VMEM((2, 1, tq), F32),
                        pltpu.VMEM((2, hw, tq), F32)],
        compiler_params=_params(("arbitrary", "arbitrary", "arbitrary")),
        name="diff_attn",
    )(*args)


def _conv_kernel(vp_ref, vc_ref, vn_ref, gp_ref, gc_ref, gn_ref, gate_ref, w_ref, b_ref, lg_ref, lb_ref,
                 o_ref, u_ref, y_ref, *, ts, nt, rb):
    i = pl.program_id(0)
    nch = u_ref.shape[0]

    def glu(v_r, g_r, c0):
        v = v_r[:, c0:c0 + HEAD_DIM].astype(F32)
        g = g_r[:, c0:c0 + HEAD_DIM].astype(F32)
        return v * jax.nn.sigmoid(g)

    has_prev = (i > 0).astype(F32)
    has_next = (i < nt - 1).astype(F32)
    for c in range(nch):
        c0 = c * HEAD_DIM
        u_ref[c, 0:CONV_HALO, :] = glu(vp_ref, gp_ref, c0) * has_prev
        u_ref[c, CONV_HALO:CONV_HALO + ts, :] = glu(vc_ref, gc_ref, c0)
        u_ref[c, CONV_HALO + ts:2 * CONV_HALO + ts, :] = glu(vn_ref, gn_ref, c0) * has_next

    base = CONV_HALO - CONV_K // 2

    def chunk(c, carry):
        for r0 in range(0, ts, rb):
            acc = jnp.zeros((rb, HEAD_DIM), F32)
            for j in range(CONV_K):
                acc = acc + w_ref[c, j:j + 1, :] * u_ref[c, pl.ds(base + r0 + j, rb), :]
            y_ref[c, r0:r0 + rb, :] = acc
        return carry

    lax.fori_loop(0, nch, chunk, 0)

    y = jnp.concatenate([y_ref[c] for c in range(nch)], -1) + b_ref[...]
    y = _ln_rows(y, CONV_LN_EPS) * lg_ref[...] + lb_ref[...]
    g = gate_ref[...].astype(F32)
    o_ref[...] = (_silu(y) * _silu(g)).astype(o_ref.dtype)


def _conv(p, conv_w, conv_b, ln_g, ln_b, seg, ts_pref):
    l = p.shape[0]
    cw = seg["cv_val"][1] - seg["cv_val"][0]
    nch = cw // HEAD_DIM
    ts = _tile(l, ts_pref)
    nt = l // ts
    hb = ts // CONV_HALO
    nhb = l // CONV_HALO
    cb = {name: seg[name][0] // cw for name in ("cv_val", "cv_glu", "cv_gate")}

    def specs(name):
        return [pl.BlockSpec((CONV_HALO, cw), lambda i: (jnp.maximum(i * hb - 1, 0), cb[name])),
                pl.BlockSpec((ts, cw), lambda i: (i, cb[name])),
                pl.BlockSpec((CONV_HALO, cw), lambda i: (jnp.minimum((i + 1) * hb, nhb - 1), cb[name]))]

    vec = pl.BlockSpec((1, cw), lambda i: (0, 0))
    w3 = jnp.transpose(conv_w.astype(F32).reshape(CONV_K, nch, HEAD_DIM), (1, 0, 2))
    return pl.pallas_call(
        functools.partial(_conv_kernel, ts=ts, nt=nt, rb=_tile(ts, 64)),
        grid=(nt,),
        in_specs=specs("cv_val") + specs("cv_glu") + [pl.BlockSpec((ts, cw), lambda i: (i, cb["cv_gate"])),
                                                      pl.BlockSpec((nch, CONV_K, HEAD_DIM), lambda i: (0, 0, 0)),
                                                      vec, vec, vec],
        out_specs=pl.BlockSpec((ts, cw), lambda i: (i, 0)),
        out_shape=jax.ShapeDtypeStruct((l, cw), BF16),
        scratch_shapes=[pltpu.VMEM((nch, ts + 2 * CONV_HALO, HEAD_DIM), F32),
                        pltpu.VMEM((nch, ts, HEAD_DIM), F32)],
        compiler_params=_params(("arbitrary",)),
        name="conformer_conv",
    )(p, p, p, p, p, p, p, w3, conv_b.reshape(1, cw).astype(F32), ln_g.reshape(1, cw).astype(F32),
      ln_b.reshape(1, cw).astype(F32))


def _merge_kernel(a1_ref, a2_ref, a3_ref, w1_ref, w2_ref, w3_ref, m1_ref, m2_ref, m3_ref, o_ref):
    def branch(a_ref, w_ref, m_ref):
        y = jnp.dot(a_ref[...], w_ref[...], preferred_element_type=F32)
        return jax.nn.sigmoid(m_ref[...].astype(F32)) * y

    o_ref[...] = (branch(a1_ref, w1_ref, m1_ref) + branch(a2_ref, w2_ref, m2_ref)
                  + branch(a3_ref, w3_ref, m3_ref)).astype(o_ref.dtype)


def _merge(a_na, a_df, a_cv, w_na, w_df, w_cv, p, seg, tm_pref, tn_pref):
    m = a_na.shape[0]
    d = w_na.shape[1]
    tm, tn = _tile(m, tm_pref), _tile(d, tn_pref)
    cb = {name: seg[name][0] // tn for name in ("merge_na", "merge_df", "merge_cv")}

    def a_spec(a):
        return pl.BlockSpec((tm, a.shape[1]), lambda i, j: (i, 0))

    def w_spec(w):
        return pl.BlockSpec((w.shape[0], tn), lambda i, j: (0, j))

    def m_spec(name):
        return pl.BlockSpec((tm, tn), lambda i, j: (i, cb[name] + j))

    return pl.pallas_call(
        _merge_kernel,
        grid=(m // tm, d // tn),
        in_specs=[a_spec(a_na), a_spec(a_df), a_spec(a_cv), w_spec(w_na), w_spec(w_df), w_spec(w_cv),
                  m_spec("merge_na"), m_spec("merge_df"), m_spec("merge_cv")],
        out_specs=pl.BlockSpec((tm, tn), lambda i, j: (i, j)),
        out_shape=jax.ShapeDtypeStruct((m, d), BF16),
        compiler_params=_params(("arbitrary", "arbitrary")),
        name="merge",
    )(a_na, a_df, a_cv, w_na, w_df, w_cv, p, p, p)


def _post_kernel(x_ref, z_ref, gate_ref, g_ref, b_ref, *rest, alpha, emit_h):
    if emit_h:
        sc_ref, sh_ref, o_ref, h_ref = rest
    else:
        (o_ref,) = rest
    v = alpha * x_ref[...] + gate_ref[...] * z_ref[...]
    y = _ln_rows(v, LN_EPS) * g_ref[...] + b_ref[...]
    o_ref[...] = y
    if emit_h:
        h_ref[...] = (_ln_rows(y, LN_EPS) * (1.0 + sc_ref[...]) + sh_ref[...]).astype(h_ref.dtype)


def _post(x, z, gate, g, b, alpha, next_mod):
    m, d = x.shape
    tm = _tile(m, 256)
    emit_h = next_mod is not None
    row = pl.BlockSpec((tm, d), lambda i: (i, 0))
    vec = pl.BlockSpec((1, d), lambda i: (0, 0))
    in_specs = [row, row, vec, vec, vec]
    args = [x, z, gate, g.reshape(1, d).astype(F32), b.reshape(1, d).astype(F32)]
    out_specs, out_shape = row, jax.ShapeDtypeStruct((m, d), F32)
    if emit_h:
        in_specs += [vec, vec]
        args += list(next_mod)
        out_specs = [row, row]
        out_shape = [out_shape, jax.ShapeDtypeStruct((m, d), BF16)]
    return pl.pallas_call(
        functools.partial(_post_kernel, alpha=alpha, emit_h=emit_h),
        grid=(m // tm,),
        in_specs=in_specs,
        out_specs=out_specs,
        out_shape=out_shape,
        compiler_params=_params(("arbitrary",)),
        name="post_ln",
    )(*args)


def _rope_tables(n_tokens):
    t = jnp.arange(n_tokens, dtype=jnp.int32)
    row = (t // GRID_W).astype(F32)
    col = (t % GRID_W).astype(F32)
    n_pairs_axis = HEAD_DIM // 4
    inv_freq = ROPE_BASE ** (-jnp.arange(n_pairs_axis, dtype=F32) / n_pairs_axis)
    ang = jnp.concatenate([row[:, None] * inv_freq, col[:, None] * inv_freq], -1)
    cos, sin = jnp.cos(ang), jnp.sin(ang)
    cosf = jnp.repeat(cos, 2, axis=-1)
    sins = jnp.stack([-sin, sin], -1).reshape(n_tokens, HEAD_DIM)
    return cosf, sins


def _segments(d, naw, dqw, dw, cw):
    sizes = (("na_q", naw), ("na_k", naw), ("na_v", naw), ("na_gate", naw),
             ("df_q", dqw), ("df_k", dqw), ("df_v", dw), ("df_gate", dw),
             ("cv_val", cw), ("cv_glu", cw), ("cv_gate", cw),
             ("merge_na", d), ("merge_df", d), ("merge_cv", d))
    seg, off = {}, 0
    for name, size in sizes:
        seg[name] = (off, off + size)
        off += size
    return seg, off


def kernel(x, c, ctx, c_ctx, w_ada, b_ada, w_in, b_in, na_rpb, diff_lq1, diff_lk1, diff_lq2, diff_lk2, diff_subln_g, conv_w, conv_b, conv_ln_g, conv_ln_b, w_proj_na, w_proj_diff, w_proj_conv, w_out, post_ln_g, post_ln_b):
    b, l, d = x.shape
    assert b == 1 and c.shape[0] == 1 and ctx.shape[0] == 1
    depth = w_ada.shape[0]
    naw, dw, cw = w_proj_na.shape[1], w_proj_diff.shape[1], w_proj_conv.shape[1]
    n_in = w_in.shape[-1]
    dqw = (n_in - 4 * naw - 2 * dw - 3 * cw - 3 * d) // 2
    seg, total = _segments(d, naw, dqw, dw, cw)
    assert total == n_in and dqw == dw
    rows = l // GRID_W
    kr = min(NA_KR, rows)
    alpha = (2.0 * depth) ** 0.25

    xl = x[0]
    xc = ctx[0]
    cc = jnp.zeros((8, d), F32).at[0].set(c[0]).at[1].set(c_ctx)
    mod = _ada(cc, w_ada, b_ada)

    def mod_vecs(i, r):
        m = mod[i, r]
        return m[None, :d], m[None, d:2 * d], m[None, 2 * d:]

    rope_tabs = _rope_tables(l)
    shift, scale, _ = mod_vecs(0, 0)
    h = _lnmod(xl, scale, shift)
    for i in range(depth):
        last = i == depth - 1
        lam_init = 0.8 - 0.6 * math.exp(-0.3 * i)
        _, _, gate = mod_vecs(i, 0)
        shift_c, scale_c, gate_c = mod_vecs(i, 1)
        w_na, w_df, w_cv = w_proj_na[i].astype(BF16), w_proj_diff[i].astype(BF16), w_proj_conv[i].astype(BF16)
        lam_params = (diff_lq1[i], diff_lk1[i], diff_lq2[i], diff_lk2[i])

        hc = _lnmod(xc, scale_c, shift_c)
        p = _inproj(h, w_in, i, b_in[i], seg, rope_tabs, 1024, 512)
        pc = _inproj(hc, w_in, i, b_in[i], seg, None, 256, 512)

        a_na = _na(p, pc, _na_bias_table(na_rpb[i], rows, kr), seg)
        a_df = _diff_attn(p, p, pc, lam_params, diff_subln_g[i], seg, lam_init, 512, 4096)
        a_cv = _conv(p, conv_w[i], conv_b[i], conv_ln_g[i], conv_ln_b[i], seg, 256)
        ym = _merge(a_na, a_df, a_cv, w_na, w_df, w_cv, p, seg, 512, 512)
        z = _matmul(ym, w_out, i, F32, 1024, 512)

        if not last:
            ac_na = _dense_attn(pc, seg)
            ac_df = _diff_attn(pc, pc, None, lam_params, diff_subln_g[i], seg, lam_init, 256, 256)
            ac_cv = _conv(pc, conv_w[i], conv_b[i], conv_ln_g[i], conv_ln_b[i], seg, 256)
            ymc = _merge(ac_na, ac_df, ac_cv, w_na, w_df, w_cv, pc, seg, 256, 512)
            zc = _matmul(ymc, w_out, i, F32, 256, 512)
            xc = _post(xc, zc, gate_c, post_ln_g[i], post_ln_b[i], alpha, None)
            shift_n, scale_n, _ = mod_vecs(i + 1, 0)
            xl, h = _post(xl, z, gate, post_ln_g[i], post_ln_b[i], alpha, (scale_n, shift_n))
        else:
            xl = _post(xl, z, gate, post_ln_g[i], post_ln_b[i], alpha, None)
    return xl[None]
```

```python
import functools
import math

import jax
import jax.numpy as jnp
from jax import lax
from jax.experimental import pallas as pl
from jax.experimental.pallas import tpu as pltpu

HEAD_DIM = 128
GRID_W = 64
NA_KR = 8
NA_KW = 16
NA_BLOCK_ROWS = 8
NA_HEADS_PER_STEP = 4
CONV_K = 31
CONV_HALO = 16
ROPE_BASE = 10000.0
LN_EPS = 1e-6
DIFF_LN_EPS = 1e-5
CONV_LN_EPS = 1e-5
NEG_BIG = -1e30
LOG2E = math.log2(math.e)
QSCALE = HEAD_DIM ** -0.5 * LOG2E
INPROJ_ROW_BLOCKS = 4
DIFF_MAX_EXCESS = 64.0
V7X_VMEM_LIMIT = 56 * 1024 * 1024

F32 = jnp.float32
BF16 = jnp.bfloat16


def _tile(n, pref):
    t = min(n, pref)
    while n % t:
        t //= 2
    return t


def _params(sem):
    return pltpu.CompilerParams(dimension_semantics=sem, vmem_limit_bytes=V7X_VMEM_LIMIT)


def _silu(x):
    return x * jax.nn.sigmoid(x)


def _ln_rows(x, eps):
    mu = jnp.mean(x, -1, keepdims=True)
    xc = x - mu
    var = jnp.mean(xc * xc, -1, keepdims=True)
    return xc * lax.rsqrt(var + eps)


def _dot_nt(a, b):
    return lax.dot_general(a, b, (((1,), (1,)), ((), ())), preferred_element_type=F32)


def _ada_kernel(cc_ref, w_ref, b_ref, o_ref):
    s = _silu(cc_ref[...])
    o_ref[0] = jnp.dot(s.astype(BF16), w_ref[0].astype(BF16), preferred_element_type=F32) + b_ref[0]


def _ada(cc, w_ada, b_ada):
    depth, d, n = w_ada.shape
    tn = _tile(n, 512)
    return pl.pallas_call(
        _ada_kernel,
        grid=(depth, n // tn),
        in_specs=[pl.BlockSpec((8, d), lambda i, j: (0, 0)),
                  pl.BlockSpec((1, d, tn), lambda i, j: (i, 0, j)),
                  pl.BlockSpec((1, 1, tn), lambda i, j: (i, 0, j))],
        out_specs=pl.BlockSpec((1, 8, tn), lambda i, j: (i, 0, j)),
        out_shape=jax.ShapeDtypeStruct((depth, 8, n), F32),
        compiler_params=_params(("arbitrary", "arbitrary")),
        name="ada",
    )(cc, w_ada, b_ada.reshape(depth, 1, n))


def _lnmod_kernel(x_ref, sc_ref, sh_ref, o_ref):
    y = _ln_rows(x_ref[...], LN_EPS)
    o_ref[...] = (y * (1.0 + sc_ref[...]) + sh_ref[...]).astype(o_ref.dtype)


def _lnmod(x, scale, shift):
    m, d = x.shape
    tm = _tile(m, 256)
    return pl.pallas_call(
        _lnmod_kernel,
        grid=(m // tm,),
        in_specs=[pl.BlockSpec((tm, d), lambda i: (i, 0)),
                  pl.BlockSpec((1, d), lambda i: (0, 0)),
                  pl.BlockSpec((1, d), lambda i: (0, 0))],
        out_specs=pl.BlockSpec((tm, d), lambda i: (i, 0)),
        out_shape=jax.ShapeDtypeStruct((m, d), BF16),
        compiler_params=_params(("arbitrary",)),
        name="lnmod",
    )(x, scale, shift)


def _rope_tile(z, cosf, sins):
    lane = lax.broadcasted_iota(jnp.int32, cosf.shape, 1)
    partner = lane ^ 1
    outs = []
    for c0 in range(0, z.shape[1], HEAD_DIM):
        zc = z[:, c0:c0 + HEAD_DIM]
        sw = jnp.take_along_axis(zc, partner, axis=1)
        outs.append(zc * cosf + sw * sins)
    return outs[0] if len(outs) == 1 else jnp.concatenate(outs, -1)


def _inproj_kernel(a_ref, w_ref, b_ref, *rest, tn, seg, qscale, rope, msplit):
    if rope:
        cos_ref, sin_ref, o_ref = rest
    else:
        (o_ref,) = rest
    col = pl.program_id(1) * tn

    def inside(name):
        return (col >= seg[name][0]) & (col < seg[name][1])

    scale = jnp.where(inside("na_q") | inside("df_q"), qscale, 1.0)
    rotate = inside("df_q") | inside("df_k")
    w = w_ref[...].astype(BF16)
    hm = a_ref.shape[0] // msplit
    for r in range(msplit):
        rows = slice(r * hm, (r + 1) * hm)
        acc = jnp.dot(a_ref[rows, :], w, preferred_element_type=F32) + b_ref[...]
        if rope:
            acc = jnp.where(rotate, _rope_tile(acc, cos_ref[rows, :], sin_ref[rows, :]), acc)
        o_ref[rows, :] = (acc * scale).astype(o_ref.dtype)


def _inproj(a, w_all, layer, b, seg, rope_tabs, tm_pref, tn_pref):
    m, k = a.shape
    n = w_all.shape[2]
    tm = _tile(m, tm_pref)
    tn = _tile(math.gcd(*[s[1] - s[0] for s in seg.values()]), tn_pref)
    rope = rope_tabs is not None
    in_specs = [pl.BlockSpec((tm, k), lambda i, j: (i, 0)),
                pl.BlockSpec((None, k, tn), lambda i, j: (layer, 0, j)),
                pl.BlockSpec((1, tn), lambda i, j: (0, j))]
    args = [a, w_all, b.reshape(1, n)]
    if rope:
        in_specs += [pl.BlockSpec((tm, HEAD_DIM), lambda i, j: (i, 0))] * 2
        args += list(rope_tabs)
    return pl.pallas_call(
        functools.partial(_inproj_kernel, tn=tn, seg=seg, qscale=QSCALE, rope=rope,
                          msplit=INPROJ_ROW_BLOCKS if tm % (INPROJ_ROW_BLOCKS * 256) == 0 else 1),
        grid=(m // tm, n // tn),
        in_specs=in_specs,
        out_specs=pl.BlockSpec((tm, tn), lambda i, j: (i, j)),
        out_shape=jax.ShapeDtypeStruct((m, n), BF16),
        compiler_params=_params(("arbitrary", "arbitrary")),
        name="inproj",
    )(*args)


def _mm_kernel(a_ref, w_ref, o_ref):
    o_ref[...] = jnp.dot(a_ref[...], w_ref[...].astype(BF16), preferred_element_type=F32).astype(o_ref.dtype)


def _matmul(a, w_all, layer, out_dtype, tm_pref, tn_pref):
    m, k = a.shape
    n = w_all.shape[2]
    tm, tn = _tile(m, tm_pref), _tile(n, tn_pref)
    return pl.pallas_call(
        _mm_kernel,
        grid=(m // tm, n // tn),
        in_specs=[pl.BlockSpec((tm, k), lambda i, j: (i, 0)),
                  pl.BlockSpec((None, k, tn), lambda i, j: (layer, 0, j))],
        out_specs=pl.BlockSpec((tm, tn), lambda i, j: (i, j)),
        out_shape=jax.ShapeDtypeStruct((m, n), out_dtype),
        compiler_params=_params(("arbitrary", "arbitrary")),
        name="outproj",
    )(a, w_all)


def _na_kernel(q_ref, *rest, nkb, nh):
    k_refs, v_refs = rest[:nkb], rest[nkb:2 * nkb]
    gate_ref, kc_ref, vc_ref, bias_ref, o_ref = rest[2 * nkb:]
    half = nkb // 2
    kh = k_refs[0].shape[0] * half
    nc = kh // HEAD_DIM
    heads = [slice(h * HEAD_DIM, (h + 1) * HEAD_DIM) for h in range(nh)]

    def halves(refs, sl):
        return [jnp.concatenate([r[:, sl] for r in refs[i * half:(i + 1) * half]], 0) for i in range(2)]

    scores = []
    for h, sl in enumerate(heads):
        q = q_ref[:, sl]
        ks = halves(k_refs, sl)
        scores.append([_dot_nt(q, ks[i]) + bias_ref[0, h, :, i * kh:(i + 1) * kh] for i in range(2)]
                      + [_dot_nt(q, kc_ref[:, sl])])
    probs, sums = [], []
    for s in scores:
        chunks = [x[:, c:c + HEAD_DIM] for x in s for c in range(0, x.shape[1], HEAD_DIM)]
        m = jnp.max(functools.reduce(jnp.maximum, chunks), -1, keepdims=True)
        ps = [jnp.exp2(ch - m) for ch in chunks]
        sums.append(jnp.sum(functools.reduce(jnp.add, ps), -1, keepdims=True))
        probs.append([jnp.concatenate([x.astype(BF16) for x in grp], -1)
                      for grp in (ps[:nc], ps[nc:2 * nc], ps[2 * nc:])])
    for h, sl in enumerate(heads):
        vs = halves(v_refs, sl)
        pb = probs[h]
        o = (jnp.dot(pb[0], vs[0], preferred_element_type=F32) + jnp.dot(pb[1], vs[1], preferred_element_type=F32)
             + jnp.dot(pb[2], vc_ref[:, sl], preferred_element_type=F32))
        g = gate_ref[:, sl].astype(F32)
        o_ref[:, sl] = (o / sums[h] * _silu(g)).astype(o_ref.dtype)


def _na_bias_table(rpb, rows, kr):
    h = rpb.shape[0]
    g, w = NA_BLOCK_ROWS, GRID_W
    ur = 2 * g
    cols = jnp.arange(w)
    col_start = jnp.clip(cols - NA_KW // 2, 0, w - NA_KW)
    kc = cols[None, :]
    valid = (kc >= col_start[:, None]) & (kc < col_start[:, None] + NA_KW)
    col_off = kc - cols[:, None] + (NA_KW - 1)
    onehot = ((col_off[None] == jnp.arange(2 * NA_KW - 1)[:, None, None]) & valid[None]).astype(F32)
    tz = jnp.einsum("hrc,cjk->hrjk", rpb.astype(F32) * LOG2E, onehot, precision=lax.Precision.HIGHEST)
    tz = jnp.where(valid[None, None], tz, NEG_BIG)
    nb = rows // g
    variants = []
    for b in (0, 1, nb - 1):
        r_blk = b * g
        u0 = min(max(r_blk - kr // 2, 0), rows - ur)
        per_row = []
        for a in range(g):
            r0 = min(max(r_blk + a - kr // 2, 0), rows - kr)
            t0 = r0 - u0
            d0 = r0 - (r_blk + a) + (NA_KR - 1)
            blk = jnp.transpose(tz[:, d0:d0 + kr], (0, 2, 1, 3)).reshape(h, w, kr * w)
            per_row.append(jnp.pad(blk, ((0, 0), (0, 0), (t0 * w, (ur - t0 - kr) * w)), constant_values=NEG_BIG))
        variants.append(jnp.stack(per_row, 1).reshape(h, g * w, ur * w))
    return jnp.stack(variants)


def _na(p, pc, bias, seg):
    l = p.shape[0]
    c = pc.shape[0]
    naw = seg["na_q"][1] - seg["na_q"][0]
    heads = naw // HEAD_DIM
    nh = NA_HEADS_PER_STEP if heads % NA_HEADS_PER_STEP == 0 else 1
    hw = nh * HEAD_DIM
    rows = l // GRID_W
    g = NA_BLOCK_ROWS
    nb = rows // g
    kr = min(NA_KR, rows)
    assert rows % g == 0 and rows >= 2 * g and kr % 2 == 0 and g % (kr // 2) == 0
    kb_rows = kr // 2
    kb = kb_rows * GRID_W
    nkb = 2 * g // kb_rows
    tq = g * GRID_W
    cb = {name: seg[name][0] // hw for name in ("na_q", "na_k", "na_v", "na_gate")}

    def u0_blk(b):
        return jnp.clip(b * g - kr // 2, 0, rows - 2 * g) // kb_rows

    def key_spec(name, i):
        return pl.BlockSpec((kb, hw), lambda h, b: (u0_blk(b) + i, cb[name] + h))

    def variant(b):
        return jnp.where(b == 0, 0, jnp.where(b == nb - 1, 2, 1))

    in_specs = ([pl.BlockSpec((tq, hw), lambda h, b: (b, cb["na_q"] + h))]
                + [key_spec("na_k", i) for i in range(nkb)]
                + [key_spec("na_v", i) for i in range(nkb)]
                + [pl.BlockSpec((tq, hw), lambda h, b: (b, cb["na_gate"] + h)),
                   pl.BlockSpec((c, hw), lambda h, b: (0, cb["na_k"] + h)),
                   pl.BlockSpec((c, hw), lambda h, b: (0, cb["na_v"] + h)),
                   pl.BlockSpec((1, nh, tq, 2 * tq), lambda h, b: (variant(b), h, 0, 0))])
    return pl.pallas_call(
        functools.partial(_na_kernel, nkb=nkb, nh=nh),
        grid=(heads // nh, nb),
        in_specs=in_specs,
        out_specs=pl.BlockSpec((tq, hw), lambda h, b: (b, h)),
        out_shape=jax.ShapeDtypeStruct((l, naw), BF16),
        compiler_params=_params(("arbitrary", "arbitrary")),
        name="natten",
    )(p, *([p] * (2 * nkb)), p, pc, pc, bias)


def _dense_kernel(q_ref, k_ref, v_ref, gate_ref, o_ref):
    s = _dot_nt(q_ref[...], k_ref[...])
    p = jnp.exp2(s - jnp.max(s, -1, keepdims=True))
    l = jnp.sum(p, -1, keepdims=True)
    o = jnp.dot(p.astype(BF16), v_ref[...], preferred_element_type=F32)
    g = gate_ref[...].astype(F32)
    o_ref[...] = (o / l * _silu(g)).astype(o_ref.dtype)


def _dense_attn(pc, seg):
    c = pc.shape[0]
    naw = seg["na_q"][1] - seg["na_q"][0]
    heads = naw // HEAD_DIM
    cb = {name: seg[name][0] // HEAD_DIM for name in ("na_q", "na_k", "na_v", "na_gate")}

    def spec(name):
        return pl.BlockSpec((c, HEAD_DIM), lambda h: (0, cb[name] + h))

    return pl.pallas_call(
        _dense_kernel,
        grid=(heads,),
        in_specs=[spec("na_q"), spec("na_k"), spec("na_v"), spec("na_gate")],
        out_specs=pl.BlockSpec((c, HEAD_DIM), lambda h: (0, h)),
        out_shape=jax.ShapeDtypeStruct((c, naw), BF16),
        compiler_params=_params(("arbitrary",)),
        name="ctx_dense_attn",
    )(pc, pc, pc, pc)


def _diff_kernel(q_ref, k_ref, v_ref, *rest, nk, lam_init, has_ctx):
    if has_ctx:
        kc_ref, vc_ref = rest[:2]
        rest = rest[2:]
    gate_ref, lq1_ref, lk1_ref, lq2_ref, lk2_ref, g_ref, o_ref, m_ref, l_ref, acc_ref = rest
    ik = pl.program_id(2)
    tq = q_ref.shape[0]
    halves = [slice(i * HEAD_DIM, (i + 1) * HEAD_DIM) for i in range(2)]

    @pl.when(ik == 0)
    def _():
        m_ref[...] = jnp.full(m_ref.shape, NEG_BIG, F32)
        l_ref[...] = jnp.zeros(l_ref.shape, F32)
        acc_ref[...] = jnp.zeros(acc_ref.shape, F32)

    def pv_t(v_r, pts):
        pt = jnp.concatenate(pts, 1)
        return lax.dot_general(v_r[...], pt, (((0,), (0,)), ((), ())), preferred_element_type=F32)

    def scores_t(k_r):
        return [_dot_nt(k_r[:, sl], q_ref[:, sl]) for sl in halves]

    def update_exact(k_r, v_r):
        st = scores_t(k_r)
        alphas, pts = [], []
        for i in range(2):
            m_old = m_ref[i]
            m_new = jnp.maximum(m_old, jnp.max(st[i], 0, keepdims=True))
            alpha = jnp.exp2(m_old - m_new)
            p = jnp.exp2(st[i] - m_new)
            l_ref[i] = alpha * l_ref[i] + jnp.sum(p, 0, keepdims=True)
            m_ref[i] = m_new
            alphas.append(alpha)
            pts.append(p.astype(BF16))
        pv = pv_t(v_r, pts)
        for i in range(2):
            acc_ref[i] = alphas[i] * acc_ref[i] + pv[:, i * tq:(i + 1) * tq]

    def update(k_r, v_r):
        st = scores_t(k_r)
        sums, pts, excess = [], [], None
        for i in range(2):
            d = st[i] - m_ref[i]
            p = jnp.exp2(d)
            sums.append(jnp.sum(p, 0, keepdims=True))
            dmax = jnp.max(d, 0, keepdims=True)
            excess = dmax if excess is None else jnp.maximum(excess, dmax)
            pts.append(p.astype(BF16))
        pv = pv_t(v_r, pts)
        ok = jnp.max(excess) <= DIFF_MAX_EXCESS

        @pl.when(ok)
        def _():
            for i in range(2):
                l_ref[i] = l_ref[i] + sums[i]
                acc_ref[i] = acc_ref[i] + pv[:, i * tq:(i + 1) * tq]

        @pl.when(jnp.logical_not(ok))
        def _():
            update_exact(k_r, v_r)

    if has_ctx:
        @pl.when(ik == 0)
        def _():
            update_exact(kc_ref, vc_ref)

    update(k_ref, v_ref)

    @pl.when(ik == nk - 1)
    def _():
        lam = (jnp.exp(jnp.sum(lq1_ref[...] * lk1_ref[...], -1, keepdims=True))
               - jnp.exp(jnp.sum(lq2_ref[...] * lk2_ref[...], -1, keepdims=True)) + lam_init)
        ot = acc_ref[0] / l_ref[0] - lam * (acc_ref[1] / l_ref[1])
        ot = ot * lax.rsqrt(jnp.mean(ot * ot, 0, keepdims=True) + DIFF_LN_EPS)
        o = ot.T * g_ref[...]
        g = gate_ref[...].astype(F32)
        o_ref[...] = (o * (1.0 - lam_init) * _silu(g)).astype(o_ref.dtype)


def _diff_attn(pq, pk, pctx, lam_params, subln_g, seg, lam_init, tq_pref, tk_pref):
    lq, lk = pq.shape[0], pk.shape[0]
    hw = 2 * HEAD_DIM
    heads = (seg["df_v"][1] - seg["df_v"][0]) // hw
    cb = {name: seg[name][0] // hw for name in ("df_q", "df_k", "df_v", "df_gate")}
    tq, tk = _tile(lq, tq_pref), _tile(lk, tk_pref)
    nq, nk = lq // tq, lk // tk
    has_ctx = pctx is not None
    in_specs = [pl.BlockSpec((tq, hw), lambda h, i, j: (i, cb["df_q"] + h)),
                pl.BlockSpec((tk, hw), lambda h, i, j: (j, cb["df_k"] + h)),
                pl.BlockSpec((tk, hw), lambda h, i, j: (j, cb["df_v"] + h))]
    args = [pq, pk, pk]
    if has_ctx:
        c = pctx.shape[0]
        in_specs += [pl.BlockSpec((c, hw), lambda h, i, j: (0, cb["df_k"] + h)),
                     pl.BlockSpec((c, hw), lambda h, i, j: (0, cb["df_v"] + h))]
        args += [pctx, pctx]
    in_specs += [pl.BlockSpec((tq, hw), lambda h, i, j: (i, cb["df_gate"] + h))]
    in_specs += [pl.BlockSpec((1, HEAD_DIM), lambda h, i, j: (0, 0))] * 4
    in_specs += [pl.BlockSpec((1, hw), lambda h, i, j: (0, 0))]
    args += [pq] + [v.reshape(1, HEAD_DIM).astype(F32) for v in lam_params] + [subln_g.reshape(1, hw).astype(F32)]
    return pl.pallas_call(
        functools.partial(_diff_kernel, nk=nk, lam_init=lam_init, has_ctx=has_ctx),
        grid=(heads, nq, nk),
        in_specs=in_specs,
        out_specs=pl.BlockSpec((tq, hw), lambda h, i, j: (i, h)),
        out_shape=jax.ShapeDtypeStruct((lq, heads * hw), BF16),
        scratch_shapes=[pltpu.VMEM((2, 1, tq), F32), pltpu.VMEM((2, 1, tq), F32),
                        pltpu.VMEM((2, hw, tq), F32)],
        compiler_params=_params(("arbitrary", "arbitrary", "arbitrary")),
        name="diff_attn",
    )(*args)


def _conv_kernel(vp_ref, vc_ref, vn_ref, gp_ref, gc_ref, gn_ref, gate_ref, w_ref, b_ref, lg_ref, lb_ref,
                 o_ref, u_ref, y_ref, *, ts, nt, rb):
    i = pl.program_id(0)
    nch = u_ref.shape[0]

    def glu(v_r, g_r, c0):
        v = v_r[:, c0:c0 + HEAD_DIM].astype(F32)
        g = g_r[:, c0:c0 + HEAD_DIM].astype(F32)
        return v * jax.nn.sigmoid(g)

    has_prev = (i > 0).astype(F32)
    has_next = (i < nt - 1).astype(F32)
    for c in range(nch):
        c0 = c * HEAD_DIM
        u_ref[c, 0:CONV_HALO, :] = glu(vp_ref, gp_ref, c0) * has_prev
        u_ref[c, CONV_HALO:CONV_HALO + ts, :] = glu(vc_ref, gc_ref, c0)
        u_ref[c, CONV_HALO + ts:2 * CONV_HALO + ts, :] = glu(vn_ref, gn_ref, c0) * has_next

    base = CONV_HALO - CONV_K // 2

    def chunk(c, carry):
        for r0 in range(0, ts, rb):
            acc = jnp.zeros((rb, HEAD_DIM), F32)
            for j in range(CONV_K):
                acc = acc + w_ref[c, j:j + 1, :] * u_ref[c, pl.ds(base + r0 + j, rb), :]
            y_ref[c, r0:r0 + rb, :] = acc
        return carry

    lax.fori_loop(0, nch, chunk, 0)

    y = jnp.concatenate([y_ref[c] for c in range(nch)], -1) + b_ref[...]
    y = _ln_rows(y, CONV_LN_EPS) * lg_ref[...] + lb_ref[...]
    g = gate_ref[...].astype(F32)
    o_ref[...] = (_silu(y) * _silu(g)).astype(o_ref.dtype)


def _conv(p, conv_w, conv_b, ln_g, ln_b, seg, ts_pref):
    l = p.shape[0]
    cw = seg["cv_val"][1] - seg["cv_val"][0]
    nch = cw // HEAD_DIM
    ts = _tile(l, ts_pref)
    nt = l // ts
    hb = ts // CONV_HALO
    nhb = l // CONV_HALO
    cb = {name: seg[name][0] // cw for name in ("cv_val", "cv_glu", "cv_gate")}

    def specs(name):
        return [pl.BlockSpec((CONV_HALO, cw), lambda i: (jnp.maximum(i * hb - 1, 0), cb[name])),
                pl.BlockSpec((ts, cw), lambda i: (i, cb[name])),
                pl.BlockSpec((CONV_HALO, cw), lambda i: (jnp.minimum((i + 1) * hb, nhb - 1), cb[name]))]

    vec = pl.BlockSpec((1, cw), lambda i: (0, 0))
    w3 = jnp.transpose(conv_w.astype(F32).reshape(CONV_K, nch, HEAD_DIM), (1, 0, 2))
    return pl.pallas_call(
        functools.partial(_conv_kernel, ts=ts, nt=nt, rb=_tile(ts, 64)),
        grid=(nt,),
        in_specs=specs("cv_val") + specs("cv_glu") + [pl.BlockSpec((ts, cw), lambda i: (i, cb["cv_gate"])),
                                                      pl.BlockSpec((nch, CONV_K, HEAD_DIM), lambda i: (0, 0, 0)),
                                                      vec, vec, vec],
        out_specs=pl.BlockSpec((ts, cw), lambda i: (i, 0)),
        out_shape=jax.ShapeDtypeStruct((l, cw), BF16),
        scratch_shapes=[pltpu.VMEM((nch, ts + 2 * CONV_HALO, HEAD_DIM), F32),
                        pltpu.VMEM((nch, ts, HEAD_DIM), F32)],
        compiler_params=_params(("arbitrary",)),
        name="conformer_conv",
    )(p, p, p, p, p, p, p, w3, conv_b.reshape(1, cw).astype(F32), ln_g.reshape(1, cw).astype(F32),
      ln_b.reshape(1, cw).astype(F32))


def _merge_kernel(a1_ref, a2_ref, a3_ref, w1_ref, w2_ref, w3_ref, m1_ref, m2_ref, m3_ref, o_ref):
    def branch(a_ref, w_ref, m_ref):
        y = jnp.dot(a_ref[...], w_ref[...], preferred_element_type=F32)
        return jax.nn.sigmoid(m_ref[...].astype(F32)) * y

    o_ref[...] = (branch(a1_ref, w1_ref, m1_ref) + branch(a2_ref, w2_ref, m2_ref)
                  + branch(a3_ref, w3_ref, m3_ref)).astype(o_ref.dtype)


def _merge(a_na, a_df, a_cv, w_na, w_df, w_cv, p, seg, tm_pref, tn_pref):
    m = a_na.shape[0]
    d = w_na.shape[1]
    tm, tn = _tile(m, tm_pref), _tile(d, tn_pref)
    cb = {name: seg[name][0] // tn for name in ("merge_na", "merge_df", "merge_cv")}

    def a_spec(a):
        return pl.BlockSpec((tm, a.shape[1]), lambda i, j: (i, 0))

    def w_spec(w):
        return pl.BlockSpec((w.shape[0], tn), lambda i, j: (0, j))

    def m_spec(name):
        return pl.BlockSpec((tm, tn), lambda i, j: (i, cb[name] + j))

    return pl.pallas_call(
        _merge_kernel,
        grid=(m // tm, d // tn),
        in_specs=[a_spec(a_na), a_spec(a_df), a_spec(a_cv), w_spec(w_na), w_spec(w_df), w_spec(w_cv),
                  m_spec("merge_na"), m_spec("merge_df"), m_spec("merge_cv")],
        out_specs=pl.BlockSpec((tm, tn), lambda i, j: (i, j)),
        out_shape=jax.ShapeDtypeStruct((m, d), BF16),
        compiler_params=_params(("arbitrary", "arbitrary")),
        name="merge",
    )(a_na, a_df, a_cv, w_na, w_df, w_cv, p, p, p)


def _post_kernel(x_ref, z_ref, gate_ref, g_ref, b_ref, *rest, alpha, emit_h):
    if emit_h:
        sc_ref, sh_ref, o_ref, h_ref = rest
    else:
        (o_ref,) = rest
    v = alpha * x_ref[...] + gate_ref[...] * z_ref[...]
    y = _ln_rows(v, LN_EPS) * g_ref[...] + b_ref[...]
    o_ref[...] = y
    if emit_h:
        h_ref[...] = (_ln_rows(y, LN_EPS) * (1.0 + sc_ref[...]) + sh_ref[...]).astype(h_ref.dtype)


def _post(x, z, gate, g, b, alpha, next_mod):
    m, d = x.shape
    tm = _tile(m, 256)
    emit_h = next_mod is not None
    row = pl.BlockSpec((tm, d), lambda i: (i, 0))
    vec = pl.BlockSpec((1, d), lambda i: (0, 0))
    in_specs = [row, row, vec, vec, vec]
    args = [x, z, gate, g.reshape(1, d).astype(F32), b.reshape(1, d).astype(F32)]
    out_specs, out_shape = row, jax.ShapeDtypeStruct((m, d), F32)
    if emit_h:
        in_specs += [vec, vec]
        args += list(next_mod)
        out_specs = [row, row]
        out_shape = [out_shape, jax.ShapeDtypeStruct((m, d), BF16)]
    return pl.pallas_call(
        functools.partial(_post_kernel, alpha=alpha, emit_h=emit_h),
        grid=(m // tm,),
        in_specs=in_specs,
        out_specs=out_specs,
        out_shape=out_shape,
        compiler_params=_params(("arbitrary",)),
        name="post_ln",
    )(*args)


def _rope_tables(n_tokens):
    t = jnp.arange(n_tokens, dtype=jnp.int32)
    row = (t // GRID_W).astype(F32)
    col = (t % GRID_W).astype(F32)
    n_pairs_axis = HEAD_DIM // 4
    inv_freq = ROPE_BASE ** (-jnp.arange(n_pairs_axis, dtype=F32) / n_pairs_axis)
    ang = jnp.concatenate([row[:, None] * inv_freq, col[:, None] * inv_freq], -1)
    cos, sin = jnp.cos(ang), jnp.sin(ang)
    cosf = jnp.repeat(cos, 2, axis=-1)
    sins = jnp.stack([-sin, sin], -1).reshape(n_tokens, HEAD_DIM)
    return cosf, sins


def _segments(d, naw, dqw, dw, cw):
    sizes = (("na_q", naw), ("na_k", naw), ("na_v", naw), ("na_gate", naw),
             ("df_q", dqw), ("df_k", dqw), ("df_v", dw), ("df_gate", dw),
             ("cv_val", cw), ("cv_glu", cw), ("cv_gate", cw),
             ("merge_na", d), ("merge_df", d), ("merge_cv", d))
    seg, off = {}, 0
    for name, size in sizes:
        seg[name] = (off, off + size)
        off += size
    return seg, off


def kernel(x, c, ctx, c_ctx, w_ada, b_ada, w_in, b_in, na_rpb, diff_lq1, diff_lk1, diff_lq2, diff_lk2, diff_subln_g, conv_w, conv_b, conv_ln_g, conv_ln_b, w_proj_na, w_proj_diff, w_proj_conv, w_out, post_ln_g, post_ln_b):
    b, l, d = x.shape
    assert b == 1 and c.shape[0] == 1 and ctx.shape[0] == 1
    depth = w_ada.shape[0]
    naw, dw, cw = w_proj_na.shape[1], w_proj_diff.shape[1], w_proj_conv.shape[1]
    n_in = w_in.shape[-1]
    dqw = (n_in - 4 * naw - 2 * dw - 3 * cw - 3 * d) // 2
    seg, total = _segments(d, naw, dqw, dw, cw)
    assert total == n_in and dqw == dw
    rows = l // GRID_W
    kr = min(NA_KR, rows)
    alpha = (2.0 * depth) ** 0.25

    xl = x[0]
    xc = ctx[0]
    cc = jnp.zeros((8, d), F32).at[0].set(c[0]).at[1].set(c_ctx)
    mod = _ada(cc, w_ada, b_ada)

    def mod_vecs(i, r):
        m = mod[i, r]
        return m[None, :d], m[None, d:2 * d], m[None, 2 * d:]

    rope_tabs = _rope_tables(l)
    shift, scale, _ = mod_vecs(0, 0)
    h = _lnmod(xl, scale, shift)
    for i in range(depth):
        last = i == depth - 1
        lam_init = 0.8 - 0.6 * math.exp(-0.3 * i)
        _, _, gate = mod_vecs(i, 0)
        shift_c, scale_c, gate_c = mod_vecs(i, 1)
        w_na, w_df, w_cv = w_proj_na[i].astype(BF16), w_proj_diff[i].astype(BF16), w_proj_conv[i].astype(BF16)
        lam_params = (diff_lq1[i], diff_lk1[i], diff_lq2[i], diff_lk2[i])

        hc = _lnmod(xc, scale_c, shift_c)
        p = _inproj(h, w_in, i, b_in[i], seg, rope_tabs, 1024, 512)
        pc = _inproj(hc, w_in, i, b_in[i], seg, None, 256, 512)

        a_na = _na(p, pc, _na_bias_table(na_rpb[i], rows, kr), seg)
        a_df = _diff_attn(p, p, pc, lam_params, diff_subln_g[i], seg, lam_init, 512, 4096)
        a_cv = _conv(p, conv_w[i], conv_b[i], conv_ln_g[i], conv_ln_b[i], seg, 256)
        ym = _merge(a_na, a_df, a_cv, w_na, w_df, w_cv, p, seg, 512, 512)
        z = _matmul(ym, w_out, i, F32, 1024, 512)

        if not last:
            ac_na = _dense_attn(pc, seg)
            ac_df = _diff_attn(pc, pc, None, lam_params, diff_subln_g[i], seg, lam_init, 256, 256)
            ac_cv = _conv(pc, conv_w[i], conv_b[i], conv_ln_g[i], conv_ln_b[i], seg, 256)
            ymc = _merge(ac_na, ac_df, ac_cv, w_na, w_df, w_cv, pc, seg, 256, 512)
            zc = _matmul(ymc, w_out, i, F32, 256, 512)
            xc = _post(xc, zc, gate_c, post_ln_g[i], post_ln_b[i], alpha, None)
            shift_n, scale_n, _ = mod_vecs(i + 1, 0)
            xl, h = _post(xl, z, gate, post_ln_g[i], post_ln_b[i], alpha, (scale_n, shift_n))
        else:
            xl = _post(xl, z, gate, post_ln_g[i], post_ln_b[i], alpha, None)
    return xl[None]
```

```python
import functools
import math

import jax
import jax.numpy as jnp
from jax import lax
from jax.experimental import pallas as pl
from jax.experimental.pallas import tpu as pltpu

HEAD_DIM = 128
GRID_W = 64
NA_KR = 8
NA_KW = 16
NA_BLOCK_ROWS = 8
NA_HEADS_PER_STEP = 4
CONV_K = 31
CONV_HALO = 16
ROPE_BASE = 10000.0
LN_EPS = 1e-6
DIFF_LN_EPS = 1e-5
CONV_LN_EPS = 1e-5
NEG_BIG = -1e30
LOG2E = math.log2(math.e)
QSCALE = HEAD_DIM ** -0.5 * LOG2E
INPROJ_ROW_BLOCK = 256
DIFF_MAX_EXCESS = 64.0
V7X_VMEM_LIMIT = 56 * 1024 * 1024

F32 = jnp.float32
BF16 = jnp.bfloat16


def _tile(n, pref):
    t = min(n, pref)
    while n % t:
        t //= 2
    return t


def _params(sem):
    return pltpu.CompilerParams(dimension_semantics=sem, vmem_limit_bytes=V7X_VMEM_LIMIT)


def _silu(x):
    return x * jax.nn.sigmoid(x)


def _ln_rows(x, eps):
    mu = jnp.mean(x, -1, keepdims=True)
    xc = x - mu
    var = jnp.mean(xc * xc, -1, keepdims=True)
    return xc * lax.rsqrt(var + eps)


def _dot_nt(a, b):
    return lax.dot_general(a, b, (((1,), (1,)), ((), ())), preferred_element_type=F32)


def _ada_kernel(cc_ref, w_ref, b_ref, o_ref):
    s = _silu(cc_ref[...])
    o_ref[0] = jnp.dot(s.astype(BF16), w_ref[0].astype(BF16), preferred_element_type=F32) + b_ref[0]


def _ada(cc, w_ada, b_ada):
    depth, d, n = w_ada.shape
    tn = _tile(n, 512)
    return pl.pallas_call(
        _ada_kernel,
        grid=(depth, n // tn),
        in_specs=[pl.BlockSpec((8, d), lambda i, j: (0, 0)),
                  pl.BlockSpec((1, d, tn), lambda i, j: (i, 0, j)),
                  pl.BlockSpec((1, 1, tn), lambda i, j: (i, 0, j))],
        out_specs=pl.BlockSpec((1, 8, tn), lambda i, j: (i, 0, j)),
        out_shape=jax.ShapeDtypeStruct((depth, 8, n), F32),
        compiler_params=_params(("arbitrary", "arbitrary")),
        name="ada",
    )(cc, w_ada, b_ada.reshape(depth, 1, n))


def _lnmod_kernel(x_ref, sc_ref, sh_ref, o_ref):
    y = _ln_rows(x_ref[...], LN_EPS)
    o_ref[...] = (y * (1.0 + sc_ref[...]) + sh_ref[...]).astype(o_ref.dtype)


def _lnmod(x, scale, shift):
    m, d = x.shape
    tm = _tile(m, 256)
    return pl.pallas_call(
        _lnmod_kernel,
        grid=(m // tm,),
        in_specs=[pl.BlockSpec((tm, d), lambda i: (i, 0)),
                  pl.BlockSpec((1, d), lambda i: (0, 0)),
                  pl.BlockSpec((1, d), lambda i: (0, 0))],
        out_specs=pl.BlockSpec((tm, d), lambda i: (i, 0)),
        out_shape=jax.ShapeDtypeStruct((m, d), BF16),
        compiler_params=_params(("arbitrary",)),
        name="lnmod",
    )(x, scale, shift)


def _rope_tile(z, cosf, sins):
    lane = lax.broadcasted_iota(jnp.int32, cosf.shape, 1)
    partner = lane ^ 1
    outs = []
    for c0 in range(0, z.shape[1], HEAD_DIM):
        zc = z[:, c0:c0 + HEAD_DIM]
        sw = jnp.take_along_axis(zc, partner, axis=1)
        outs.append(zc * cosf + sw * sins)
    return outs[0] if len(outs) == 1 else jnp.concatenate(outs, -1)


def _inproj_kernel(a_ref, w_ref, b_ref, *rest, tn, seg, qscale, rope, msplit):
    if rope:
        cos_ref, sin_ref, o_ref = rest
    else:
        (o_ref,) = rest
    col = pl.program_id(1) * tn

    def inside(name):
        return (col >= seg[name][0]) & (col < seg[name][1])

    scale = jnp.where(inside("na_q") | inside("df_q"), qscale, 1.0)
    rotate = inside("df_q") | inside("df_k")
    w = w_ref[...].astype(BF16)
    hm = a_ref.shape[0] // msplit
    for r in range(msplit):
        rows = slice(r * hm, (r + 1) * hm)
        acc = jnp.dot(a_ref[rows, :], w, preferred_element_type=F32) + b_ref[...]
        if rope:
            acc = jnp.where(rotate, _rope_tile(acc, cos_ref[rows, :], sin_ref[rows, :]), acc)
        o_ref[rows, :] = (acc * scale).astype(o_ref.dtype)


def _inproj(a, w_all, layer, b, seg, rope_tabs, tm_pref, tn_pref):
    m, k = a.shape
    n = w_all.shape[2]
    tm = _tile(m, tm_pref)
    tn = _tile(math.gcd(*[s[1] - s[0] for s in seg.values()]), tn_pref)
    rope = rope_tabs is not None
    in_specs = [pl.BlockSpec((tm, k), lambda i, j: (i, 0), pipeline_mode=pl.Buffered(1)),
                pl.BlockSpec((None, k, tn), lambda i, j: (layer, 0, j)),
                pl.BlockSpec((1, tn), lambda i, j: (0, j))]
    args = [a, w_all, b.reshape(1, n)]
    if rope:
        in_specs += [pl.BlockSpec((tm, HEAD_DIM), lambda i, j: (i, 0))] * 2
        args += list(rope_tabs)
    return pl.pallas_call(
        functools.partial(_inproj_kernel, tn=tn, seg=seg, qscale=QSCALE, rope=rope,
                          msplit=tm // INPROJ_ROW_BLOCK if tm % (4 * INPROJ_ROW_BLOCK) == 0 else 1),
        grid=(m // tm, n // tn),
        in_specs=in_specs,
        out_specs=pl.BlockSpec((tm, tn), lambda i, j: (i, j)),
        out_shape=jax.ShapeDtypeStruct((m, n), BF16),
        compiler_params=_params(("arbitrary", "arbitrary")),
        name="inproj",
    )(*args)


def _mm_kernel(a_ref, w_ref, o_ref):
    o_ref[...] = jnp.dot(a_ref[...], w_ref[...].astype(BF16), preferred_element_type=F32).astype(o_ref.dtype)


def _matmul(a, w_all, layer, out_dtype, tm_pref, tn_pref):
    m, k = a.shape
    n = w_all.shape[2]
    tm, tn = _tile(m, tm_pref), _tile(n, tn_pref)
    return pl.pallas_call(
        _mm_kernel,
        grid=(m // tm, n // tn),
        in_specs=[pl.BlockSpec((tm, k), lambda i, j: (i, 0), pipeline_mode=pl.Buffered(1)),
                  pl.BlockSpec((None, k, tn), lambda i, j: (layer, 0, j))],
        out_specs=pl.BlockSpec((tm, tn), lambda i, j: (i, j)),
        out_shape=jax.ShapeDtypeStruct((m, n), out_dtype),
        compiler_params=_params(("arbitrary", "arbitrary")),
        name="outproj",
    )(a, w_all)


def _na_kernel(q_ref, *rest, nkb, nh):
    k_refs, v_refs = rest[:nkb], rest[nkb:2 * nkb]
    gate_ref, kc_ref, vc_ref, bias_ref, o_ref = rest[2 * nkb:]
    half = nkb // 2
    kh = k_refs[0].shape[0] * half
    nc = kh // HEAD_DIM
    heads = [slice(h * HEAD_DIM, (h + 1) * HEAD_DIM) for h in range(nh)]

    def halves(refs, sl):
        return [jnp.concatenate([r[:, sl] for r in refs[i * half:(i + 1) * half]], 0) for i in range(2)]

    scores = []
    for h, sl in enumerate(heads):
        q = q_ref[:, sl]
        ks = halves(k_refs, sl)
        scores.append([_dot_nt(q, ks[i]) + bias_ref[0, h, :, i * kh:(i + 1) * kh] for i in range(2)]
                      + [_dot_nt(q, kc_ref[:, sl])])
    probs, sums = [], []
    for s in scores:
        chunks = [x[:, c:c + HEAD_DIM] for x in s for c in range(0, x.shape[1], HEAD_DIM)]
        m = jnp.max(functools.reduce(jnp.maximum, chunks), -1, keepdims=True)
        ps = [jnp.exp2(ch - m) for ch in chunks]
        sums.append(jnp.sum(functools.reduce(jnp.add, ps), -1, keepdims=True))
        probs.append([jnp.concatenate([x.astype(BF16) for x in grp], -1)
                      for grp in (ps[:nc], ps[nc:2 * nc], ps[2 * nc:])])
    for h, sl in enumerate(heads):
        vs = halves(v_refs, sl)
        pb = probs[h]
        o = (jnp.dot(pb[0], vs[0], preferred_element_type=F32) + jnp.dot(pb[1], vs[1], preferred_element_type=F32)
             + jnp.dot(pb[2], vc_ref[:, sl], preferred_element_type=F32))
        g = gate_ref[:, sl].astype(F32)
        o_ref[:, sl] = (o / sums[h] * _silu(g)).astype(o_ref.dtype)


def _na_bias_table(rpb, rows, kr):
    h = rpb.shape[0]
    g, w = NA_BLOCK_ROWS, GRID_W
    ur = 2 * g
    cols = jnp.arange(w)
    col_start = jnp.clip(cols - NA_KW // 2, 0, w - NA_KW)
    kc = cols[None, :]
    valid = (kc >= col_start[:, None]) & (kc < col_start[:, None] + NA_KW)
    col_off = kc - cols[:, None] + (NA_KW - 1)
    onehot = ((col_off[None] == jnp.arange(2 * NA_KW - 1)[:, None, None]) & valid[None]).astype(F32)
    tz = jnp.einsum("hrc,cjk->hrjk", rpb.astype(F32) * LOG2E, onehot, precision=lax.Precision.HIGHEST)
    tz = jnp.where(valid[None, None], tz, NEG_BIG)
    nb = rows // g
    variants = []
    for b in (0, 1, nb - 1):
        r_blk = b * g
        u0 = min(max(r_blk - kr // 2, 0), rows - ur)
        per_row = []
        for a in range(g):
            r0 = min(max(r_blk + a - kr // 2, 0), rows - kr)
            t0 = r0 - u0
            d0 = r0 - (r_blk + a) + (NA_KR - 1)
            blk = jnp.transpose(tz[:, d0:d0 + kr], (0, 2, 1, 3)).reshape(h, w, kr * w)
            per_row.append(jnp.pad(blk, ((0, 0), (0, 0), (t0 * w, (ur - t0 - kr) * w)), constant_values=NEG_BIG))
        variants.append(jnp.stack(per_row, 1).reshape(h, g * w, ur * w))
    return jnp.stack(variants)


def _na(p, pc, bias, seg):
    l = p.shape[0]
    c = pc.shape[0]
    naw = seg["na_q"][1] - seg["na_q"][0]
    heads = naw // HEAD_DIM
    nh = NA_HEADS_PER_STEP if heads % NA_HEADS_PER_STEP == 0 else 1
    hw = nh * HEAD_DIM
    rows = l // GRID_W
    g = NA_BLOCK_ROWS
    nb = rows // g
    kr = min(NA_KR, rows)
    assert rows % g == 0 and rows >= 2 * g and kr % 2 == 0 and g % (kr // 2) == 0
    kb_rows = kr // 2
    kb = kb_rows * GRID_W
    nkb = 2 * g // kb_rows
    tq = g * GRID_W
    cb = {name: seg[name][0] // hw for name in ("na_q", "na_k", "na_v", "na_gate")}

    def u0_blk(b):
        return jnp.clip(b * g - kr // 2, 0, rows - 2 * g) // kb_rows

    def key_spec(name, i):
        return pl.BlockSpec((kb, hw), lambda h, b: (u0_blk(b) + i, cb[name] + h))

    def variant(b):
        return jnp.where(b == 0, 0, jnp.where(b == nb - 1, 2, 1))

    in_specs = ([pl.BlockSpec((tq, hw), lambda h, b: (b, cb["na_q"] + h))]
                + [key_spec("na_k", i) for i in range(nkb)]
                + [key_spec("na_v", i) for i in range(nkb)]
                + [pl.BlockSpec((tq, hw), lambda h, b: (b, cb["na_gate"] + h)),
                   pl.BlockSpec((c, hw), lambda h, b: (0, cb["na_k"] + h)),
                   pl.BlockSpec((c, hw), lambda h, b: (0, cb["na_v"] + h)),
                   pl.BlockSpec((1, nh, tq, 2 * tq), lambda h, b: (variant(b), h, 0, 0))])
    return pl.pallas_call(
        functools.partial(_na_kernel, nkb=nkb, nh=nh),
        grid=(heads // nh, nb),
        in_specs=in_specs,
        out_specs=pl.BlockSpec((tq, hw), lambda h, b: (b, h)),
        out_shape=jax.ShapeDtypeStruct((l, naw), BF16),
        compiler_params=_params(("arbitrary", "arbitrary")),
        name="natten",
    )(p, *([p] * (2 * nkb)), p, pc, pc, bias)


def _dense_kernel(q_ref, k_ref, v_ref, gate_ref, o_ref):
    s = _dot_nt(q_ref[...], k_ref[...])
    p = jnp.exp2(s - jnp.max(s, -1, keepdims=True))
    l = jnp.sum(p, -1, keepdims=True)
    o = jnp.dot(p.astype(BF16), v_ref[...], preferred_element_type=F32)
    g = gate_ref[...].astype(F32)
    o_ref[...] = (o / l * _silu(g)).astype(o_ref.dtype)


def _dense_attn(pc, seg):
    c = pc.shape[0]
    naw = seg["na_q"][1] - seg["na_q"][0]
    heads = naw // HEAD_DIM
    cb = {name: seg[name][0] // HEAD_DIM for name in ("na_q", "na_k", "na_v", "na_gate")}

    def spec(name):
        return pl.BlockSpec((c, HEAD_DIM), lambda h: (0, cb[name] + h))

    return pl.pallas_call(
        _dense_kernel,
        grid=(heads,),
        in_specs=[spec("na_q"), spec("na_k"), spec("na_v"), spec("na_gate")],
        out_specs=pl.BlockSpec((c, HEAD_DIM), lambda h: (0, h)),
        out_shape=jax.ShapeDtypeStruct((c, naw), BF16),
        compiler_params=_params(("arbitrary",)),
        name="ctx_dense_attn",
    )(pc, pc, pc, pc)


def _diff_kernel(q_ref, k_ref, v_ref, *rest, nk, lam_init, has_ctx):
    if has_ctx:
        kc_ref, vc_ref = rest[:2]
        rest = rest[2:]
    gate_ref, lq1_ref, lk1_ref, lq2_ref, lk2_ref, g_ref, o_ref, m_ref, l_ref, acc_ref = rest
    ik = pl.program_id(2)
    tq = q_ref.shape[0]
    halves = [slice(i * HEAD_DIM, (i + 1) * HEAD_DIM) for i in range(2)]

    @pl.when(ik == 0)
    def _():
        m_ref[...] = jnp.full(m_ref.shape, NEG_BIG, F32)
        l_ref[...] = jnp.zeros(l_ref.shape, F32)
        acc_ref[...] = jnp.zeros(acc_ref.shape, F32)

    def pv_t(v_r, pts):
        pt = jnp.concatenate(pts, 1)
        return lax.dot_general(v_r[...], pt, (((0,), (0,)), ((), ())), preferred_element_type=F32)

    def scores_t(k_r):
        return [_dot_nt(k_r[:, sl], q_ref[:, sl]) for sl in halves]

    def update_exact(k_r, v_r):
        st = scores_t(k_r)
        alphas, pts = [], []
        for i in range(2):
            m_old = m_ref[i]
            m_new = jnp.maximum(m_old, jnp.max(st[i], 0, keepdims=True))
            alpha = jnp.exp2(m_old - m_new)
            p = jnp.exp2(st[i] - m_new)
            l_ref[i] = alpha * l_ref[i] + jnp.sum(p, 0, keepdims=True)
            m_ref[i] = m_new
            alphas.append(alpha)
            pts.append(p.astype(BF16))
        pv = pv_t(v_r, pts)
        for i in range(2):
            acc_ref[i] = alphas[i] * acc_ref[i] + pv[:, i * tq:(i + 1) * tq]

    def update(k_r, v_r):
        st = scores_t(k_r)
        sums, pts, excess = [], [], None
        for i in range(2):
            d = st[i] - m_ref[i]
            p = jnp.exp2(d)
            sums.append(jnp.sum(p, 0, keepdims=True))
            dmax = jnp.max(d, 0, keepdims=True)
            excess = dmax if excess is None else jnp.maximum(excess, dmax)
            pts.append(p.astype(BF16))
        pv = pv_t(v_r, pts)
        ok = jnp.max(excess) <= DIFF_MAX_EXCESS

        @pl.when(ok)
        def _():
            for i in range(2):
                l_ref[i] = l_ref[i] + sums[i]
                acc_ref[i] = acc_ref[i] + pv[:, i * tq:(i + 1) * tq]

        @pl.when(jnp.logical_not(ok))
        def _():
            update_exact(k_r, v_r)

    if has_ctx:
        @pl.when(ik == 0)
        def _():
            update_exact(kc_ref, vc_ref)

    update(k_ref, v_ref)

    @pl.when(ik == nk - 1)
    def _():
        lam = (jnp.exp(jnp.sum(lq1_ref[...] * lk1_ref[...], -1, keepdims=True))
               - jnp.exp(jnp.sum(lq2_ref[...] * lk2_ref[...], -1, keepdims=True)) + lam_init)
        ot = acc_ref[0] / l_ref[0] - lam * (acc_ref[1] / l_ref[1])
        ot = ot * lax.rsqrt(jnp.mean(ot * ot, 0, keepdims=True) + DIFF_LN_EPS)
        o = ot.T * g_ref[...]
        g = gate_ref[...].astype(F32)
        o_ref[...] = (o * (1.0 - lam_init) * _silu(g)).astype(o_ref.dtype)


def _diff_attn(pq, pk, pctx, lam_params, subln_g, seg, lam_init, tq_pref, tk_pref):
    lq, lk = pq.shape[0], pk.shape[0]
    hw = 2 * HEAD_DIM
    heads = (seg["df_v"][1] - seg["df_v"][0]) // hw
    cb = {name: seg[name][0] // hw for name in ("df_q", "df_k", "df_v", "df_gate")}
    tq, tk = _tile(lq, tq_pref), _tile(lk, tk_pref)
    nq, nk = lq // tq, lk // tk
    has_ctx = pctx is not None
    in_specs = [pl.BlockSpec((tq, hw), lambda h, i, j: (i, cb["df_q"] + h)),
                pl.BlockSpec((tk, hw), lambda h, i, j: (j, cb["df_k"] + h)),
                pl.BlockSpec((tk, hw), lambda h, i, j: (j, cb["df_v"] + h))]
    args = [pq, pk, pk]
    if has_ctx:
        c = pctx.shape[0]
        in_specs += [pl.BlockSpec((c, hw), lambda h, i, j: (0, cb["df_k"] + h)),
                     pl.BlockSpec((c, hw), lambda h, i, j: (0, cb["df_v"] + h))]
        args += [pctx, pctx]
    in_specs += [pl.BlockSpec((tq, hw), lambda h, i, j: (i, cb["df_gate"] + h))]
    in_specs += [pl.BlockSpec((1, HEAD_DIM), lambda h, i, j: (0, 0))] * 4
    in_specs += [pl.BlockSpec((1, hw), lambda h, i, j: (0, 0))]
    args += [pq] + [v.reshape(1, HEAD_DIM).astype(F32) for v in lam_params] + [subln_g.reshape(1, hw).astype(F32)]
    return pl.pallas_call(
        functools.partial(_diff_kernel, nk=nk, lam_init=lam_init, has_ctx=has_ctx),
        grid=(heads, nq, nk),
        in_specs=in_specs,
        out_specs=pl.BlockSpec((tq, hw), lambda h, i, j: (i, h)),
        out_shape=jax.ShapeDtypeStruct((lq, heads * hw), BF16),
        scratch_shapes=[pltpu.VMEM((2, 1, tq), F32), pltpu.VMEM((2, 1, tq), F32),
                        pltpu.VMEM((2, hw, tq), F32)],
        compiler_params=_params(("arbitrary", "arbitrary", "arbitrary")),
        name="diff_attn",
    )(*args)


def _conv_kernel(vp_ref, vc_ref, vn_ref, gp_ref, gc_ref, gn_ref, gate_ref, w_ref, b_ref, lg_ref, lb_ref,
                 o_ref, u_ref, y_ref, *, ts, nt, rb):
    i = pl.program_id(0)
    nch = u_ref.shape[0]

    def glu(v_r, g_r, c0):
        v = v_r[:, c0:c0 + HEAD_DIM].astype(F32)
        g = g_r[:, c0:c0 + HEAD_DIM].astype(F32)
        return v * jax.nn.sigmoid(g)

    has_prev = (i > 0).astype(F32)
    has_next = (i < nt - 1).astype(F32)
    for c in range(nch):
        c0 = c * HEAD_DIM
        u_ref[c, 0:CONV_HALO, :] = glu(vp_ref, gp_ref, c0) * has_prev
        u_ref[c, CONV_HALO:CONV_HALO + ts, :] = glu(vc_ref, gc_ref, c0)
        u_ref[c, CONV_HALO + ts:2 * CONV_HALO + ts, :] = glu(vn_ref, gn_ref, c0) * has_next

    base = CONV_HALO - CONV_K // 2

    def chunk(c, carry):
        for r0 in range(0, ts, rb):
            acc = jnp.zeros((rb, HEAD_DIM), F32)
            for j in range(CONV_K):
                acc = acc + w_ref[c, j:j + 1, :] * u_ref[c, pl.ds(base + r0 + j, rb), :]
            y_ref[c, r0:r0 + rb, :] = acc
        return carry

    lax.fori_loop(0, nch, chunk, 0)

    y = jnp.concatenate([y_ref[c] for c in range(nch)], -1) + b_ref[...]
    y = _ln_rows(y, CONV_LN_EPS) * lg_ref[...] + lb_ref[...]
    g = gate_ref[...].astype(F32)
    o_ref[...] = (_silu(y) * _silu(g)).astype(o_ref.dtype)


def _conv(p, conv_w, conv_b, ln_g, ln_b, seg, ts_pref):
    l = p.shape[0]
    cw = seg["cv_val"][1] - seg["cv_val"][0]
    nch = cw // HEAD_DIM
    ts = _tile(l, ts_pref)
    nt = l // ts
    hb = ts // CONV_HALO
    nhb = l // CONV_HALO
    cb = {name: seg[name][0] // cw for name in ("cv_val", "cv_glu", "cv_gate")}

    def specs(name):
        return [pl.BlockSpec((CONV_HALO, cw), lambda i: (jnp.maximum(i * hb - 1, 0), cb[name])),
                pl.BlockSpec((ts, cw), lambda i: (i, cb[name])),
                pl.BlockSpec((CONV_HALO, cw), lambda i: (jnp.minimum((i + 1) * hb, nhb - 1), cb[name]))]

    vec = pl.BlockSpec((1, cw), lambda i: (0, 0))
    w3 = jnp.transpose(conv_w.astype(F32).reshape(CONV_K, nch, HEAD_DIM), (1, 0, 2))
    return pl.pallas_call(
        functools.partial(_conv_kernel, ts=ts, nt=nt, rb=_tile(ts, 64)),
        grid=(nt,),
        in_specs=specs("cv_val") + specs("cv_glu") + [pl.BlockSpec((ts, cw), lambda i: (i, cb["cv_gate"])),
                                                      pl.BlockSpec((nch, CONV_K, HEAD_DIM), lambda i: (0, 0, 0)),
                                                      vec, vec, vec],
        out_specs=pl.BlockSpec((ts, cw), lambda i: (i, 0)),
        out_shape=jax.ShapeDtypeStruct((l, cw), BF16),
        scratch_shapes=[pltpu.VMEM((nch, ts + 2 * CONV_HALO, HEAD_DIM), F32),
                        pltpu.VMEM((nch, ts, HEAD_DIM), F32)],
        compiler_params=_params(("arbitrary",)),
        name="conformer_conv",
    )(p, p, p, p, p, p, p, w3, conv_b.reshape(1, cw).astype(F32), ln_g.reshape(1, cw).astype(F32),
      ln_b.reshape(1, cw).astype(F32))


def _merge_kernel(a1_ref, a2_ref, a3_ref, w1_ref, w2_ref, w3_ref, m1_ref, m2_ref, m3_ref, o_ref):
    def branch(a_ref, w_ref, m_ref):
        y = jnp.dot(a_ref[...], w_ref[...], preferred_element_type=F32)
        return jax.nn.sigmoid(m_ref[...].astype(F32)) * y

    o_ref[...] = (branch(a1_ref, w1_ref, m1_ref) + branch(a2_ref, w2_ref, m2_ref)
                  + branch(a3_ref, w3_ref, m3_ref)).astype(o_ref.dtype)


def _merge(a_na, a_df, a_cv, w_na, w_df, w_cv, p, seg, tm_pref, tn_pref):
    m = a_na.shape[0]
    d = w_na.shape[1]
    tm, tn = _tile(m, tm_pref), _tile(d, tn_pref)
    cb = {name: seg[name][0] // tn for name in ("merge_na", "merge_df", "merge_cv")}

    def a_spec(a):
        return pl.BlockSpec((tm, a.shape[1]), lambda i, j: (i, 0))

    def w_spec(w):
        return pl.BlockSpec((w.shape[0], tn), lambda i, j: (0, j))

    def m_spec(name):
        return pl.BlockSpec((tm, tn), lambda i, j: (i, cb[name] + j))

    return pl.pallas_call(
        _merge_kernel,
        grid=(m // tm, d // tn),
        in_specs=[a_spec(a_na), a_spec(a_df), a_spec(a_cv), w_spec(w_na), w_spec(w_df), w_spec(w_cv),
                  m_spec("merge_na"), m_spec("merge_df"), m_spec("merge_cv")],
        out_specs=pl.BlockSpec((tm, tn), lambda i, j: (i, j)),
        out_shape=jax.ShapeDtypeStruct((m, d), BF16),
        compiler_params=_params(("arbitrary", "arbitrary")),
        name="merge",
    )(a_na, a_df, a_cv, w_na, w_df, w_cv, p, p, p)


def _post_kernel(x_ref, z_ref, gate_ref, g_ref, b_ref, *rest, alpha, emit_h):
    if emit_h:
        sc_ref, sh_ref, o_ref, h_ref = rest
    else:
        (o_ref,) = rest
    v = alpha * x_ref[...] + gate_ref[...] * z_ref[...]
    y = _ln_rows(v, LN_EPS) * g_ref[...] + b_ref[...]
    o_ref[...] = y
    if emit_h:
        h_ref[...] = (_ln_rows(y, LN_EPS) * (1.0 + sc_ref[...]) + sh_ref[...]).astype(h_ref.dtype)


def _post(x, z, gate, g, b, alpha, next_mod):
    m, d = x.shape
    tm = _tile(m, 256)
    emit_h = next_mod is not None
    row = pl.BlockSpec((tm, d), lambda i: (i, 0))
    vec = pl.BlockSpec((1, d), lambda i: (0, 0))
    in_specs = [row, row, vec, vec, vec]
    args = [x, z, gate, g.reshape(1, d).astype(F32), b.reshape(1, d).astype(F32)]
    out_specs, out_shape = row, jax.ShapeDtypeStruct((m, d), F32)
    if emit_h:
        in_specs += [vec, vec]
        args += list(next_mod)
        out_specs = [row, row]
        out_shape = [out_shape, jax.ShapeDtypeStruct((m, d), BF16)]
    return pl.pallas_call(
        functools.partial(_post_kernel, alpha=alpha, emit_h=emit_h),
        grid=(m // tm,),
        in_specs=in_specs,
        out_specs=out_specs,
        out_shape=out_shape,
        compiler_params=_params(("arbitrary",)),
        name="post_ln",
    )(*args)


def _rope_tables(n_tokens):
    t = jnp.arange(n_tokens, dtype=jnp.int32)
    row = (t // GRID_W).astype(F32)
    col = (t % GRID_W).astype(F32)
    n_pairs_axis = HEAD_DIM // 4
    inv_freq = ROPE_BASE ** (-jnp.arange(n_pairs_axis, dtype=F32) / n_pairs_axis)
    ang = jnp.concatenate([row[:, None] * inv_freq, col[:, None] * inv_freq], -1)
    cos, sin = jnp.cos(ang), jnp.sin(ang)
    cosf = jnp.repeat(cos, 2, axis=-1)
    sins = jnp.stack([-sin, sin], -1).reshape(n_tokens, HEAD_DIM)
    return cosf, sins


def _segments(d, naw, dqw, dw, cw):
    sizes = (("na_q", naw), ("na_k", naw), ("na_v", naw), ("na_gate", naw),
             ("df_q", dqw), ("df_k", dqw), ("df_v", dw), ("df_gate", dw),
             ("cv_val", cw), ("cv_glu", cw), ("cv_gate", cw),
             ("merge_na", d), ("merge_df", d), ("merge_cv", d))
    seg, off = {}, 0
    for name, size in sizes:
        seg[name] = (off, off + size)
        off += size
    return seg, off


def kernel(x, c, ctx, c_ctx, w_ada, b_ada, w_in, b_in, na_rpb, diff_lq1, diff_lk1, diff_lq2, diff_lk2, diff_subln_g, conv_w, conv_b, conv_ln_g, conv_ln_b, w_proj_na, w_proj_diff, w_proj_conv, w_out, post_ln_g, post_ln_b):
    b, l, d = x.shape
    assert b == 1 and c.shape[0] == 1 and ctx.shape[0] == 1
    depth = w_ada.shape[0]
    naw, dw, cw = w_proj_na.shape[1], w_proj_diff.shape[1], w_proj_conv.shape[1]
    n_in = w_in.shape[-1]
    dqw = (n_in - 4 * naw - 2 * dw - 3 * cw - 3 * d) // 2
    seg, total = _segments(d, naw, dqw, dw, cw)
    assert total == n_in and dqw == dw
    rows = l // GRID_W
    kr = min(NA_KR, rows)
    alpha = (2.0 * depth) ** 0.25

    xl = x[0]
    xc = ctx[0]
    cc = jnp.zeros((8, d), F32).at[0].set(c[0]).at[1].set(c_ctx)
    mod = _ada(cc, w_ada, b_ada)

    def mod_vecs(i, r):
        m = mod[i, r]
        return m[None, :d], m[None, d:2 * d], m[None, 2 * d:]

    rope_tabs = _rope_tables(l)
    shift, scale, _ = mod_vecs(0, 0)
    h = _lnmod(xl, scale, shift)
    for i in range(depth):
        last = i == depth - 1
        lam_init = 0.8 - 0.6 * math.exp(-0.3 * i)
        _, _, gate = mod_vecs(i, 0)
        shift_c, scale_c, gate_c = mod_vecs(i, 1)
        w_na, w_df, w_cv = w_proj_na[i].astype(BF16), w_proj_diff[i].astype(BF16), w_proj_conv[i].astype(BF16)
        lam_params = (diff_lq1[i], diff_lk1[i], diff_lq2[i], diff_lk2[i])

        hc = _lnmod(xc, scale_c, shift_c)
        p = _inproj(h, w_in, i, b_in[i], seg, rope_tabs, 2048, 512)
        pc = _inproj(hc, w_in, i, b_in[i], seg, None, 256, 512)

        a_na = _na(p, pc, _na_bias_table(na_rpb[i], rows, kr), seg)
        a_df = _diff_attn(p, p, pc, lam_params, diff_subln_g[i], seg, lam_init, 512, 4096)
        a_cv = _conv(p, conv_w[i], conv_b[i], conv_ln_g[i], conv_ln_b[i], seg, 256)
        ym = _merge(a_na, a_df, a_cv, w_na, w_df, w_cv, p, seg, 512, 512)
        z = _matmul(ym, w_out, i, F32, 2048, 512)

        if not last:
            ac_na = _dense_attn(pc, seg)
            ac_df = _diff_attn(pc, pc, None, lam_params, diff_subln_g[i], seg, lam_init, 256, 256)
            ac_cv = _conv(pc, conv_w[i], conv_b[i], conv_ln_g[i], conv_ln_b[i], seg, 256)
            ymc = _merge(ac_na, ac_df, ac_cv, w_na, w_df, w_cv, pc, seg, 256, 512)
            zc = _matmul(ymc, w_out, i, F32, 256, 512)
            xc = _post(xc, zc, gate_c, post_ln_g[i], post_ln_b[i], alpha, None)
            shift_n, scale_n, _ = mod_vecs(i + 1, 0)
            xl, h = _post(xl, z, gate, post_ln_g[i], post_ln_b[i], alpha, (scale_n, shift_n))
        else:
            xl = _post(xl, z, gate, post_ln_g[i], post_ln_b[i], alpha, None)
    return xl[None]
```

```python
import functools
import math

import jax
import jax.numpy as jnp
from jax import lax
from jax.experimental import pallas as pl
from jax.experimental.pallas import tpu as pltpu

HEAD_DIM = 128
GRID_W = 64
NA_KR = 8
NA_KW = 16
NA_BLOCK_ROWS = 8
NA_HEADS_PER_STEP = 4
CONV_K = 31
CONV_HALO = 16
ROPE_BASE = 10000.0
LN_EPS = 1e-6
DIFF_LN_EPS = 1e-5
CONV_LN_EPS = 1e-5
NEG_BIG = -1e30
LOG2E = math.log2(math.e)
QSCALE = HEAD_DIM ** -0.5 * LOG2E
INPROJ_ROW_BLOCK = 256
DIFF_MAX_EXCESS = 64.0
V7X_VMEM_LIMIT = 56 * 1024 * 1024

F32 = jnp.float32
BF16 = jnp.bfloat16


def _tile(n, pref):
    t = min(n, pref)
    while n % t:
        t //= 2
    return t


def _params(sem):
    return pltpu.CompilerParams(dimension_semantics=sem, vmem_limit_bytes=V7X_VMEM_LIMIT)


def _silu(x):
    return x * jax.nn.sigmoid(x)


def _ln_rows(x, eps):
    mu = jnp.mean(x, -1, keepdims=True)
    xc = x - mu
    var = jnp.mean(xc * xc, -1, keepdims=True)
    return xc * lax.rsqrt(var + eps)


def _dot_nt(a, b):
    return lax.dot_general(a, b, (((1,), (1,)), ((), ())), preferred_element_type=F32)


def _ada_kernel(cc_ref, w_ref, b_ref, o_ref):
    s = _silu(cc_ref[...])
    o_ref[0] = jnp.dot(s.astype(BF16), w_ref[0].astype(BF16), preferred_element_type=F32) + b_ref[0]


def _ada(cc, w_ada, b_ada):
    depth, d, n = w_ada.shape
    tn = _tile(n, 512)
    return pl.pallas_call(
        _ada_kernel,
        grid=(depth, n // tn),
        in_specs=[pl.BlockSpec((8, d), lambda i, j: (0, 0)),
                  pl.BlockSpec((1, d, tn), lambda i, j: (i, 0, j)),
                  pl.BlockSpec((1, 1, tn), lambda i, j: (i, 0, j))],
        out_specs=pl.BlockSpec((1, 8, tn), lambda i, j: (i, 0, j)),
        out_shape=jax.ShapeDtypeStruct((depth, 8, n), F32),
        compiler_params=_params(("arbitrary", "arbitrary")),
        name="ada",
    )(cc, w_ada, b_ada.reshape(depth, 1, n))


def _lnmod_kernel(x_ref, sc_ref, sh_ref, o_ref):
    y = _ln_rows(x_ref[...], LN_EPS)
    o_ref[...] = (y * (1.0 + sc_ref[...]) + sh_ref[...]).astype(o_ref.dtype)


def _lnmod(x, scale, shift):
    m, d = x.shape
    tm = _tile(m, 256)
    return pl.pallas_call(
        _lnmod_kernel,
        grid=(m // tm,),
        in_specs=[pl.BlockSpec((tm, d), lambda i: (i, 0)),
                  pl.BlockSpec((1, d), lambda i: (0, 0)),
                  pl.BlockSpec((1, d), lambda i: (0, 0))],
        out_specs=pl.BlockSpec((tm, d), lambda i: (i, 0)),
        out_shape=jax.ShapeDtypeStruct((m, d), BF16),
        compiler_params=_params(("arbitrary",)),
        name="lnmod",
    )(x, scale, shift)


def _rope_tile(z, cosf, sins):
    lane = lax.broadcasted_iota(jnp.int32, cosf.shape, 1)
    partner = lane ^ 1
    outs = []
    for c0 in range(0, z.shape[1], HEAD_DIM):
        zc = z[:, c0:c0 + HEAD_DIM]
        sw = jnp.take_along_axis(zc, partner, axis=1)
        outs.append(zc * cosf + sw * sins)
    return outs[0] if len(outs) == 1 else jnp.concatenate(outs, -1)


def _inproj_kernel(a_ref, w_ref, b_ref, *rest, tn, seg, qscale, rope, msplit, src_tile):
    if rope:
        cos_ref, sin_ref, o_ref = rest
    else:
        (o_ref,) = rest
    col = src_tile(pl.program_id(1)) * tn

    def inside(name):
        return (col >= seg[name][0]) & (col < seg[name][1])

    scale = jnp.where(inside("na_q") | inside("df_q"), qscale, 1.0)
    rotate = inside("df_q") | inside("df_k")
    w = w_ref[...].astype(BF16)
    hm = a_ref.shape[0] // msplit
    for r in range(msplit):
        rows = slice(r * hm, (r + 1) * hm)
        acc = jnp.dot(a_ref[rows, :], w, preferred_element_type=F32) + b_ref[...]
        if rope:
            acc = jnp.where(rotate, _rope_tile(acc, cos_ref[rows, :], sin_ref[rows, :]), acc)
        o_ref[rows, :] = (acc * scale).astype(o_ref.dtype)


def _inproj(a, w_all, layer, b, seg, rope_tabs, tm_pref, tn_pref, only=None):
    m, k = a.shape
    n = w_all.shape[2]
    tm = _tile(m, tm_pref)
    tn = _tile(math.gcd(*[s[1] - s[0] for s in seg.values()]), tn_pref)
    rope = rope_tabs is not None
    names = tuple(seg) if only is None else tuple(only)
    seg_out, tile_ranges, off = {}, [], 0
    for name in names:
        s0, s1 = seg[name]
        seg_out[name] = (off, off + s1 - s0)
        tile_ranges.append((s0 // tn, off // tn))
        off += s1 - s0
    n_out = off

    def src_tile(j):
        t = j
        for s_tile, o_tile in tile_ranges:
            t = jnp.where(j >= o_tile, s_tile + (j - o_tile), t)
        return t

    in_specs = [pl.BlockSpec((tm, k), lambda i, j: (i, 0), pipeline_mode=pl.Buffered(1)),
                pl.BlockSpec((None, k, tn), lambda i, j: (layer, 0, src_tile(j))),
                pl.BlockSpec((1, tn), lambda i, j: (0, src_tile(j)))]
    args = [a, w_all, b.reshape(1, n)]
    if rope:
        in_specs += [pl.BlockSpec((tm, HEAD_DIM), lambda i, j: (i, 0))] * 2
        args += list(rope_tabs)
    out = pl.pallas_call(
        functools.partial(_inproj_kernel, tn=tn, seg=seg, qscale=QSCALE, rope=rope, src_tile=src_tile,
                          msplit=tm // INPROJ_ROW_BLOCK if tm % (4 * INPROJ_ROW_BLOCK) == 0 else 1),
        grid=(m // tm, n_out // tn),
        in_specs=in_specs,
        out_specs=pl.BlockSpec((tm, tn), lambda i, j: (i, j)),
        out_shape=jax.ShapeDtypeStruct((m, n_out), BF16),
        compiler_params=_params(("arbitrary", "arbitrary")),
        name="inproj",
    )(*args)
    return out, seg_out


def _mm_kernel(a_ref, w_ref, o_ref):
    o_ref[...] = jnp.dot(a_ref[...], w_ref[...].astype(BF16), preferred_element_type=F32).astype(o_ref.dtype)


def _matmul(a, w_all, layer, out_dtype, tm_pref, tn_pref):
    m, k = a.shape
    n = w_all.shape[2]
    tm, tn = _tile(m, tm_pref), _tile(n, tn_pref)
    return pl.pallas_call(
        _mm_kernel,
        grid=(m // tm, n // tn),
        in_specs=[pl.BlockSpec((tm, k), lambda i, j: (i, 0), pipeline_mode=pl.Buffered(1)),
                  pl.BlockSpec((None, k, tn), lambda i, j: (layer, 0, j))],
        out_specs=pl.BlockSpec((tm, tn), lambda i, j: (i, j)),
        out_shape=jax.ShapeDtypeStruct((m, n), out_dtype),
        compiler_params=_params(("arbitrary", "arbitrary")),
        name="outproj",
    )(a, w_all)


def _na_kernel(q_ref, *rest, nkb, nh):
    k_refs, v_refs = rest[:nkb], rest[nkb:2 * nkb]
    gate_ref, kc_ref, vc_ref, bias_ref, o_ref = rest[2 * nkb:]
    half = nkb // 2
    kh = k_refs[0].shape[0] * half
    nc = kh // HEAD_DIM
    heads = [slice(h * HEAD_DIM, (h + 1) * HEAD_DIM) for h in range(nh)]

    def halves(refs, sl):
        return [jnp.concatenate([r[:, sl] for r in refs[i * half:(i + 1) * half]], 0) for i in range(2)]

    scores = []
    for h, sl in enumerate(heads):
        q = q_ref[:, sl]
        ks = halves(k_refs, sl)
        scores.append([_dot_nt(q, ks[i]) + bias_ref[0, h, :, i * kh:(i + 1) * kh] for i in range(2)]
                      + [_dot_nt(q, kc_ref[:, sl])])
    probs, sums = [], []
    for s in scores:
        chunks = [x[:, c:c + HEAD_DIM] for x in s for c in range(0, x.shape[1], HEAD_DIM)]
        m = jnp.max(functools.reduce(jnp.maximum, chunks), -1, keepdims=True)
        ps = [jnp.exp2(ch - m) for ch in chunks]
        sums.append(jnp.sum(functools.reduce(jnp.add, ps), -1, keepdims=True))
        probs.append([jnp.concatenate([x.astype(BF16) for x in grp], -1)
                      for grp in (ps[:nc], ps[nc:2 * nc], ps[2 * nc:])])
    for h, sl in enumerate(heads):
        vs = halves(v_refs, sl)
        pb = probs[h]
        o = (jnp.dot(pb[0], vs[0], preferred_element_type=F32) + jnp.dot(pb[1], vs[1], preferred_element_type=F32)
             + jnp.dot(pb[2], vc_ref[:, sl], preferred_element_type=F32))
        g = gate_ref[:, sl].astype(F32)
        o_ref[:, sl] = (o / sums[h] * _silu(g)).astype(o_ref.dtype)


def _na_bias_table(rpb, rows, kr):
    h = rpb.shape[0]
    g, w = NA_BLOCK_ROWS, GRID_W
    ur = 2 * g
    cols = jnp.arange(w)
    col_start = jnp.clip(cols - NA_KW // 2, 0, w - NA_KW)
    kc = cols[None, :]
    valid = (kc >= col_start[:, None]) & (kc < col_start[:, None] + NA_KW)
    col_off = kc - cols[:, None] + (NA_KW - 1)
    onehot = ((col_off[None] == jnp.arange(2 * NA_KW - 1)[:, None, None]) & valid[None]).astype(F32)
    tz = jnp.einsum("hrc,cjk->hrjk", rpb.astype(F32) * LOG2E, onehot, precision=lax.Precision.HIGHEST)
    tz = jnp.where(valid[None, None], tz, NEG_BIG)
    nb = rows // g
    variants = []
    for b in (0, 1, nb - 1):
        r_blk = b * g
        u0 = min(max(r_blk - kr // 2, 0), rows - ur)
        per_row = []
        for a in range(g):
            r0 = min(max(r_blk + a - kr // 2, 0), rows - kr)
            t0 = r0 - u0
            d0 = r0 - (r_blk + a) + (NA_KR - 1)
            blk = jnp.transpose(tz[:, d0:d0 + kr], (0, 2, 1, 3)).reshape(h, w, kr * w)
            per_row.append(jnp.pad(blk, ((0, 0), (0, 0), (t0 * w, (ur - t0 - kr) * w)), constant_values=NEG_BIG))
        variants.append(jnp.stack(per_row, 1).reshape(h, g * w, ur * w))
    return jnp.stack(variants)


def _na(p, pc, seg_ctx, bias, seg):
    l = p.shape[0]
    c = pc.shape[0]
    naw = seg["na_q"][1] - seg["na_q"][0]
    heads = naw // HEAD_DIM
    nh = NA_HEADS_PER_STEP if heads % NA_HEADS_PER_STEP == 0 else 1
    hw = nh * HEAD_DIM
    rows = l // GRID_W
    g = NA_BLOCK_ROWS
    nb = rows // g
    kr = min(NA_KR, rows)
    assert rows % g == 0 and rows >= 2 * g and kr % 2 == 0 and g % (kr // 2) == 0
    kb_rows = kr // 2
    kb = kb_rows * GRID_W
    nkb = 2 * g // kb_rows
    tq = g * GRID_W
    cb = {name: seg[name][0] // hw for name in ("na_q", "na_k", "na_v", "na_gate")}

    def u0_blk(b):
        return jnp.clip(b * g - kr // 2, 0, rows - 2 * g) // kb_rows

    def key_spec(name, i):
        return pl.BlockSpec((kb, hw), lambda h, b: (u0_blk(b) + i, cb[name] + h))

    def variant(b):
        return jnp.where(b == 0, 0, jnp.where(b == nb - 1, 2, 1))

    in_specs = ([pl.BlockSpec((tq, hw), lambda h, b: (b, cb["na_q"] + h))]
                + [key_spec("na_k", i) for i in range(nkb)]
                + [key_spec("na_v", i) for i in range(nkb)]
                + [pl.BlockSpec((tq, hw), lambda h, b: (b, cb["na_gate"] + h)),
                   pl.BlockSpec((c, hw), lambda h, b: (0, seg_ctx["na_k"][0] // hw + h)),
                   pl.BlockSpec((c, hw), lambda h, b: (0, seg_ctx["na_v"][0] // hw + h)),
                   pl.BlockSpec((1, nh, tq, 2 * tq), lambda h, b: (variant(b), h, 0, 0))])
    return pl.pallas_call(
        functools.partial(_na_kernel, nkb=nkb, nh=nh),
        grid=(heads // nh, nb),
        in_specs=in_specs,
        out_specs=pl.BlockSpec((tq, hw), lambda h, b: (b, h)),
        out_shape=jax.ShapeDtypeStruct((l, naw), BF16),
        compiler_params=_params(("arbitrary", "arbitrary")),
        name="natten",
    )(p, *([p] * (2 * nkb)), p, pc, pc, bias)


def _dense_kernel(q_ref, k_ref, v_ref, gate_ref, o_ref):
    s = _dot_nt(q_ref[...], k_ref[...])
    p = jnp.exp2(s - jnp.max(s, -1, keepdims=True))
    l = jnp.sum(p, -1, keepdims=True)
    o = jnp.dot(p.astype(BF16), v_ref[...], preferred_element_type=F32)
    g = gate_ref[...].astype(F32)
    o_ref[...] = (o / l * _silu(g)).astype(o_ref.dtype)


def _dense_attn(pc, seg):
    c = pc.shape[0]
    naw = seg["na_q"][1] - seg["na_q"][0]
    heads = naw // HEAD_DIM
    cb = {name: seg[name][0] // HEAD_DIM for name in ("na_q", "na_k", "na_v", "na_gate")}

    def spec(name):
        return pl.BlockSpec((c, HEAD_DIM), lambda h: (0, cb[name] + h))

    return pl.pallas_call(
        _dense_kernel,
        grid=(heads,),
        in_specs=[spec("na_q"), spec("na_k"), spec("na_v"), spec("na_gate")],
        out_specs=pl.BlockSpec((c, HEAD_DIM), lambda h: (0, h)),
        out_shape=jax.ShapeDtypeStruct((c, naw), BF16),
        compiler_params=_params(("arbitrary",)),
        name="ctx_dense_attn",
    )(pc, pc, pc, pc)


def _diff_kernel(q_ref, k_ref, v_ref, *rest, nk, lam_init, has_ctx):
    if has_ctx:
        kc_ref, vc_ref = rest[:2]
        rest = rest[2:]
    gate_ref, lq1_ref, lk1_ref, lq2_ref, lk2_ref, g_ref, o_ref, m_ref, l_ref, acc_ref = rest
    ik = pl.program_id(2)
    tq = q_ref.shape[0]
    halves = [slice(i * HEAD_DIM, (i + 1) * HEAD_DIM) for i in range(2)]

    @pl.when(ik == 0)
    def _():
        m_ref[...] = jnp.full(m_ref.shape, NEG_BIG, F32)
        l_ref[...] = jnp.zeros(l_ref.shape, F32)
        acc_ref[...] = jnp.zeros(acc_ref.shape, F32)

    def pv_t(v_r, pts):
        pt = jnp.concatenate(pts, 1)
        return lax.dot_general(v_r[...], pt, (((0,), (0,)), ((), ())), preferred_element_type=F32)

    def scores_t(k_r):
        return [_dot_nt(k_r[:, sl], q_ref[:, sl]) for sl in halves]

    def update_exact(k_r, v_r):
        st = scores_t(k_r)
        alphas, pts = [], []
        for i in range(2):
            m_old = m_ref[i]
            m_new = jnp.maximum(m_old, jnp.max(st[i], 0, keepdims=True))
            alpha = jnp.exp2(m_old - m_new)
            p = jnp.exp2(st[i] - m_new)
            l_ref[i] = alpha * l_ref[i] + jnp.sum(p, 0, keepdims=True)
            m_ref[i] = m_new
            alphas.append(alpha)
            pts.append(p.astype(BF16))
        pv = pv_t(v_r, pts)
        for i in range(2):
            acc_ref[i] = alphas[i] * acc_ref[i] + pv[:, i * tq:(i + 1) * tq]

    def update(k_r, v_r):
        st = scores_t(k_r)
        sums, pts, excess = [], [], None
        for i in range(2):
            d = st[i] - m_ref[i]
            p = jnp.exp2(d)
            sums.append(jnp.sum(p, 0, keepdims=True))
            dmax = jnp.max(d, 0, keepdims=True)
            excess = dmax if excess is None else jnp.maximum(excess, dmax)
            pts.append(p.astype(BF16))
        pv = pv_t(v_r, pts)
        ok = jnp.max(excess) <= DIFF_MAX_EXCESS

        @pl.when(ok)
        def _():
            for i in range(2):
                l_ref[i] = l_ref[i] + sums[i]
                acc_ref[i] = acc_ref[i] + pv[:, i * tq:(i + 1) * tq]

        @pl.when(jnp.logical_not(ok))
        def _():
            update_exact(k_r, v_r)

    if has_ctx:
        @pl.when(ik == 0)
        def _():
            update_exact(kc_ref, vc_ref)

    update(k_ref, v_ref)

    @pl.when(ik == nk - 1)
    def _():
        lam = (jnp.exp(jnp.sum(lq1_ref[...] * lk1_ref[...], -1, keepdims=True))
               - jnp.exp(jnp.sum(lq2_ref[...] * lk2_ref[...], -1, keepdims=True)) + lam_init)
        ot = acc_ref[0] / l_ref[0] - lam * (acc_ref[1] / l_ref[1])
        ot = ot * lax.rsqrt(jnp.mean(ot * ot, 0, keepdims=True) + DIFF_LN_EPS)
        o = ot.T * g_ref[...]
        g = gate_ref[...].astype(F32)
        o_ref[...] = (o * (1.0 - lam_init) * _silu(g)).astype(o_ref.dtype)


def _diff_attn(pq, pk, pctx, seg_ctx, lam_params, subln_g, seg, lam_init, tq_pref, tk_pref):
    lq, lk = pq.shape[0], pk.shape[0]
    hw = 2 * HEAD_DIM
    heads = (seg["df_v"][1] - seg["df_v"][0]) // hw
    cb = {name: seg[name][0] // hw for name in ("df_q", "df_k", "df_v", "df_gate")}
    tq, tk = _tile(lq, tq_pref), _tile(lk, tk_pref)
    nq, nk = lq // tq, lk // tk
    has_ctx = pctx is not None
    in_specs = [pl.BlockSpec((tq, hw), lambda h, i, j: (i, cb["df_q"] + h)),
                pl.BlockSpec((tk, hw), lambda h, i, j: (j, cb["df_k"] + h)),
                pl.BlockSpec((tk, hw), lambda h, i, j: (j, cb["df_v"] + h))]
    args = [pq, pk, pk]
    if has_ctx:
        c = pctx.shape[0]
        ckc, cvc = seg_ctx["df_k"][0] // hw, seg_ctx["df_v"][0] // hw
        in_specs += [pl.BlockSpec((c, hw), lambda h, i, j: (0, ckc + h)),
                     pl.BlockSpec((c, hw), lambda h, i, j: (0, cvc + h))]
        args += [pctx, pctx]
    in_specs += [pl.BlockSpec((tq, hw), lambda h, i, j: (i, cb["df_gate"] + h))]
    in_specs += [pl.BlockSpec((1, HEAD_DIM), lambda h, i, j: (0, 0))] * 4
    in_specs += [pl.BlockSpec((1, hw), lambda h, i, j: (0, 0))]
    args += [pq] + [v.reshape(1, HEAD_DIM).astype(F32) for v in lam_params] + [subln_g.reshape(1, hw).astype(F32)]
    return pl.pallas_call(
        functools.partial(_diff_kernel, nk=nk, lam_init=lam_init, has_ctx=has_ctx),
        grid=(heads, nq, nk),
        in_specs=in_specs,
        out_specs=pl.BlockSpec((tq, hw), lambda h, i, j: (i, h)),
        out_shape=jax.ShapeDtypeStruct((lq, heads * hw), BF16),
        scratch_shapes=[pltpu.VMEM((2, 1, tq), F32), pltpu.VMEM((2, 1, tq), F32),
                        pltpu.VMEM((2, hw, tq), F32)],
        compiler_params=_params(("arbitrary", "arbitrary", "arbitrary")),
        name="diff_attn",
    )(*args)


def _conv_kernel(vp_ref, vc_ref, vn_ref, gp_ref, gc_ref, gn_ref, gate_ref, w_ref, b_ref, lg_ref, lb_ref,
                 o_ref, u_ref, y_ref, *, ts, nt, rb):
    i = pl.program_id(0)
    nch = u_ref.shape[0]

    def glu(v_r, g_r, c0):
        v = v_r[:, c0:c0 + HEAD_DIM].astype(F32)
        g = g_r[:, c0:c0 + HEAD_DIM].astype(F32)
        return v * jax.nn.sigmoid(g)

    has_prev = (i > 0).astype(F32)
    has_next = (i < nt - 1).astype(F32)
    for c in range(nch):
        c0 = c * HEAD_DIM
        u_ref[c, 0:CONV_HALO, :] = glu(vp_ref, gp_ref, c0) * has_prev
        u_ref[c, CONV_HALO:CONV_HALO + ts, :] = glu(vc_ref, gc_ref, c0)
        u_ref[c, CONV_HALO + ts:2 * CONV_HALO + ts, :] = glu(vn_ref, gn_ref, c0) * has_next

    base = CONV_HALO - CONV_K // 2

    def chunk(c, carry):
        for r0 in range(0, ts, rb):
            acc = jnp.zeros((rb, HEAD_DIM), F32)
            for j in range(CONV_K):
                acc = acc + w_ref[c, j:j + 1, :] * u_ref[c, pl.ds(base + r0 + j, rb), :]
            y_ref[c, r0:r0 + rb, :] = acc
        return carry

    lax.fori_loop(0, nch, chunk, 0)

    y = jnp.concatenate([y_ref[c] for c in range(nch)], -1) + b_ref[...]
    y = _ln_rows(y, CONV_LN_EPS) * lg_ref[...] + lb_ref[...]
    g = gate_ref[...].astype(F32)
    o_ref[...] = (_silu(y) * _silu(g)).astype(o_ref.dtype)


def _conv(p, conv_w, conv_b, ln_g, ln_b, seg, ts_pref):
    l = p.shape[0]
    cw = seg["cv_val"][1] - seg["cv_val"][0]
    nch = cw // HEAD_DIM
    ts = _tile(l, ts_pref)
    nt = l // ts
    hb = ts // CONV_HALO
    nhb = l // CONV_HALO
    cb = {name: seg[name][0] // cw for name in ("cv_val", "cv_glu", "cv_gate")}

    def specs(name):
        return [pl.BlockSpec((CONV_HALO, cw), lambda i: (jnp.maximum(i * hb - 1, 0), cb[name])),
                pl.BlockSpec((ts, cw), lambda i: (i, cb[name])),
                pl.BlockSpec((CONV_HALO, cw), lambda i: (jnp.minimum((i + 1) * hb, nhb - 1), cb[name]))]

    vec = pl.BlockSpec((1, cw), lambda i: (0, 0))
    w3 = jnp.transpose(conv_w.astype(F32).reshape(CONV_K, nch, HEAD_DIM), (1, 0, 2))
    return pl.pallas_call(
        functools.partial(_conv_kernel, ts=ts, nt=nt, rb=_tile(ts, 64)),
        grid=(nt,),
        in_specs=specs("cv_val") + specs("cv_glu") + [pl.BlockSpec((ts, cw), lambda i: (i, cb["cv_gate"])),
                                                      pl.BlockSpec((nch, CONV_K, HEAD_DIM), lambda i: (0, 0, 0)),
                                                      vec, vec, vec],
        out_specs=pl.BlockSpec((ts, cw), lambda i: (i, 0)),
        out_shape=jax.ShapeDtypeStruct((l, cw), BF16),
        scratch_shapes=[pltpu.VMEM((nch, ts + 2 * CONV_HALO, HEAD_DIM), F32),
                        pltpu.VMEM((nch, ts, HEAD_DIM), F32)],
        compiler_params=_params(("arbitrary",)),
        name="conformer_conv",
    )(p, p, p, p, p, p, p, w3, conv_b.reshape(1, cw).astype(F32), ln_g.reshape(1, cw).astype(F32),
      ln_b.reshape(1, cw).astype(F32))


def _merge_kernel(a1_ref, a2_ref, a3_ref, w1_ref, w2_ref, w3_ref, m1_ref, m2_ref, m3_ref, o_ref):
    def branch(a_ref, w_ref, m_ref):
        y = jnp.dot(a_ref[...], w_ref[...], preferred_element_type=F32)
        return jax.nn.sigmoid(m_ref[...].astype(F32)) * y

    o_ref[...] = (branch(a1_ref, w1_ref, m1_ref) + branch(a2_ref, w2_ref, m2_ref)
                  + branch(a3_ref, w3_ref, m3_ref)).astype(o_ref.dtype)


def _merge(a_na, a_df, a_cv, w_na, w_df, w_cv, p, seg, tm_pref, tn_pref):
    m = a_na.shape[0]
    d = w_na.shape[1]
    tm, tn = _tile(m, tm_pref), _tile(d, tn_pref)
    cb = {name: seg[name][0] // tn for name in ("merge_na", "merge_df", "merge_cv")}

    def a_spec(a):
        return pl.BlockSpec((tm, a.shape[1]), lambda i, j: (i, 0))

    def w_spec(w):
        return pl.BlockSpec((w.shape[0], tn), lambda i, j: (0, j))

    def m_spec(name):
        return pl.BlockSpec((tm, tn), lambda i, j: (i, cb[name] + j))

    return pl.pallas_call(
        _merge_kernel,
        grid=(m // tm, d // tn),
        in_specs=[a_spec(a_na), a_spec(a_df), a_spec(a_cv), w_spec(w_na), w_spec(w_df), w_spec(w_cv),
                  m_spec("merge_na"), m_spec("merge_df"), m_spec("merge_cv")],
        out_specs=pl.BlockSpec((tm, tn), lambda i, j: (i, j)),
        out_shape=jax.ShapeDtypeStruct((m, d), BF16),
        compiler_params=_params(("arbitrary", "arbitrary")),
        name="merge",
    )(a_na, a_df, a_cv, w_na, w_df, w_cv, p, p, p)


def _post_kernel(x_ref, z_ref, gate_ref, g_ref, b_ref, *rest, alpha, emit_h):
    if emit_h:
        sc_ref, sh_ref, o_ref, h_ref = rest
    else:
        (o_ref,) = rest
    v = alpha * x_ref[...] + gate_ref[...] * z_ref[...]
    y = _ln_rows(v, LN_EPS) * g_ref[...] + b_ref[...]
    o_ref[...] = y
    if emit_h:
        h_ref[...] = (_ln_rows(y, LN_EPS) * (1.0 + sc_ref[...]) + sh_ref[...]).astype(h_ref.dtype)


def _post(x, z, gate, g, b, alpha, next_mod):
    m, d = x.shape
    tm = _tile(m, 256)
    emit_h = next_mod is not None
    row = pl.BlockSpec((tm, d), lambda i: (i, 0))
    vec = pl.BlockSpec((1, d), lambda i: (0, 0))
    in_specs = [row, row, vec, vec, vec]
    args = [x, z, gate, g.reshape(1, d).astype(F32), b.reshape(1, d).astype(F32)]
    out_specs, out_shape = row, jax.ShapeDtypeStruct((m, d), F32)
    if emit_h:
        in_specs += [vec, vec]
        args += list(next_mod)
        out_specs = [row, row]
        out_shape = [out_shape, jax.ShapeDtypeStruct((m, d), BF16)]
    return pl.pallas_call(
        functools.partial(_post_kernel, alpha=alpha, emit_h=emit_h),
        grid=(m // tm,),
        in_specs=in_specs,
        out_specs=out_specs,
        out_shape=out_shape,
        compiler_params=_params(("arbitrary",)),
        name="post_ln",
    )(*args)


def _rope_tables(n_tokens):
    t = jnp.arange(n_tokens, dtype=jnp.int32)
    row = (t // GRID_W).astype(F32)
    col = (t % GRID_W).astype(F32)
    n_pairs_axis = HEAD_DIM // 4
    inv_freq = ROPE_BASE ** (-jnp.arange(n_pairs_axis, dtype=F32) / n_pairs_axis)
    ang = jnp.concatenate([row[:, None] * inv_freq, col[:, None] * inv_freq], -1)
    cos, sin = jnp.cos(ang), jnp.sin(ang)
    cosf = jnp.repeat(cos, 2, axis=-1)
    sins = jnp.stack([-sin, sin], -1).reshape(n_tokens, HEAD_DIM)
    return cosf, sins


def _segments(d, naw, dqw, dw, cw):
    sizes = (("na_q", naw), ("na_k", naw), ("na_v", naw), ("na_gate", naw),
             ("df_q", dqw), ("df_k", dqw), ("df_v", dw), ("df_gate", dw),
             ("cv_val", cw), ("cv_glu", cw), ("cv_gate", cw),
             ("merge_na", d), ("merge_df", d), ("merge_cv", d))
    seg, off = {}, 0
    for name, size in sizes:
        seg[name] = (off, off + size)
        off += size
    return seg, off


def kernel(x, c, ctx, c_ctx, w_ada, b_ada, w_in, b_in, na_rpb, diff_lq1, diff_lk1, diff_lq2, diff_lk2, diff_subln_g, conv_w, conv_b, conv_ln_g, conv_ln_b, w_proj_na, w_proj_diff, w_proj_conv, w_out, post_ln_g, post_ln_b):
    b, l, d = x.shape
    assert b == 1 and c.shape[0] == 1 and ctx.shape[0] == 1
    depth = w_ada.shape[0]
    naw, dw, cw = w_proj_na.shape[1], w_proj_diff.shape[1], w_proj_conv.shape[1]
    n_in = w_in.shape[-1]
    dqw = (n_in - 4 * naw - 2 * dw - 3 * cw - 3 * d) // 2
    seg, total = _segments(d, naw, dqw, dw, cw)
    assert total == n_in and dqw == dw
    rows = l // GRID_W
    kr = min(NA_KR, rows)
    alpha = (2.0 * depth) ** 0.25

    xl = x[0]
    xc = ctx[0]
    cc = jnp.zeros((8, d), F32).at[0].set(c[0]).at[1].set(c_ctx)
    mod = _ada(cc, w_ada, b_ada)

    def mod_vecs(i, r):
        m = mod[i, r]
        return m[None, :d], m[None, d:2 * d], m[None, 2 * d:]

    rope_tabs = _rope_tables(l)
    shift, scale, _ = mod_vecs(0, 0)
    h = _lnmod(xl, scale, shift)
    for i in range(depth):
        last = i == depth - 1
        lam_init = 0.8 - 0.6 * math.exp(-0.3 * i)
        _, _, gate = mod_vecs(i, 0)
        shift_c, scale_c, gate_c = mod_vecs(i, 1)
        w_na, w_df, w_cv = w_proj_na[i].astype(BF16), w_proj_diff[i].astype(BF16), w_proj_conv[i].astype(BF16)
        lam_params = (diff_lq1[i], diff_lk1[i], diff_lq2[i], diff_lk2[i])

        hc = _lnmod(xc, scale_c, shift_c)
        p, _ = _inproj(h, w_in, i, b_in[i], seg, rope_tabs, 2048, 512)
        pc, seg_c = _inproj(hc, w_in, i, b_in[i], seg, None, 256, 512,
                            only=("na_k", "na_v", "df_k", "df_v") if last else None)

        a_na = _na(p, pc, seg_c, _na_bias_table(na_rpb[i], rows, kr), seg)
        a_df = _diff_attn(p, p, pc, seg_c, lam_params, diff_subln_g[i], seg, lam_init, 512, 4096)
        a_cv = _conv(p, conv_w[i], conv_b[i], conv_ln_g[i], conv_ln_b[i], seg, 256)
        ym = _merge(a_na, a_df, a_cv, w_na, w_df, w_cv, p, seg, 512, 512)
        z = _matmul(ym, w_out, i, F32, 2048, 512)

        if not last:
            ac_na = _dense_attn(pc, seg)
            ac_df = _diff_attn(pc, pc, None, None, lam_params, diff_subln_g[i], seg, lam_init, 256, 256)
            ac_cv = _conv(pc, conv_w[i], conv_b[i], conv_ln_g[i], conv_ln_b[i], seg, 256)
            ymc = _merge(ac_na, ac_df, ac_cv, w_na, w_df, w_cv, pc, seg, 256, 512)
            zc = _matmul(ymc, w_out, i, F32, 256, 512)
            xc = _post(xc, zc, gate_c, post_ln_g[i], post_ln_b[i], alpha, None)
            shift_n, scale_n, _ = mod_vecs(i + 1, 0)
            xl, h = _post(xl, z, gate, post_ln_g[i], post_ln_b[i], alpha, (scale_n, shift_n))
        else:
            xl = _post(xl, z, gate, post_ln_g[i], post_ln_b[i], alpha, None)
    return xl[None]
```
